```python
import jax, jax.numpy as jnp
from jax import lax
import numpy as np

D_MODEL = 1024
BATCH = 2
SEQ = 8192
DEPTH = 1
DEC_BATCH = 32
DEC_SEQ = 32
PAST_LEN = 1024

CHUNK = 64
N_HEADS = 8
N_KV_HEADS = 2
HEAD_DIM = 128
N_IDX_HEADS = 8
IDX_DIM = 64
TOPK_MAX = 256
Q_BLOCK = 128
ROPE_THETA = 500000.0
ROT_FRACTION = 4
MLP_CHUNK = 128
N_SGU_GROUPS = 8
SGU_WIDTH = D_MODEL
SGU_GROUP_DIM = SGU_WIDTH // N_SGU_GROUPS
FFN_HIDDEN = ((8 * D_MODEL // 3 + 255) // 256) * 256
INDEX_SCALE = (IDX_DIM * N_IDX_HEADS) ** -0.5
NEG = -1e30

IN_WIDTHS = [N_HEADS * HEAD_DIM,
             N_KV_HEADS * HEAD_DIM,
             N_KV_HEADS * HEAD_DIM,
             N_IDX_HEADS * IDX_DIM,
             IDX_DIM,
             N_IDX_HEADS,
             SGU_WIDTH,
             SGU_WIDTH,
             2 * D_MODEL]
N_IN = sum(IN_WIDTHS)
SPLIT_POINTS = [int(v) for v in np.cumsum(IN_WIDTHS)[:-1]]

kernel_name = 'hybrid_dsa_gmlp_stream_encoder'


def _rmsnorm(x, g, eps=1e-6):
    xf = x.astype(jnp.float32)
    y = xf * lax.rsqrt(jnp.mean(xf * xf, axis=-1, keepdims=True) + eps)
    return (y * g.astype(jnp.float32)).astype(x.dtype)


def _layernorm(x, g, eps=1e-6):
    xf = x.astype(jnp.float32)
    mu = jnp.mean(xf, axis=-1, keepdims=True)
    xc = xf - mu
    y = xc * lax.rsqrt(jnp.mean(xc * xc, axis=-1, keepdims=True) + eps)
    return (y * g.astype(jnp.float32)).astype(x.dtype)


def _rope(x, pos):
    d = x.shape[-1]
    rd = d // ROT_FRACTION
    half = rd // 2
    inv = jnp.power(ROPE_THETA, -(jnp.arange(half, dtype=jnp.float32) / half))
    ang = pos.astype(jnp.float32)[:, None] * inv[None, :]
    cos = jnp.cos(ang)[None, :, None, :]
    sin = jnp.sin(ang)[None, :, None, :]
    xf = x.astype(jnp.float32)
    x1, x2, rest = xf[..., :half], xf[..., half:rd], xf[..., rd:]
    out = jnp.concatenate([x1 * cos - x2 * sin, x2 * cos + x1 * sin, rest], axis=-1)
    return out.astype(x.dtype)


def _dsa_block(q, qi, wi, qpos, k_all, v_all, ki_all, kpos, topk):
    B, Tq = q.shape[0], q.shape[1]
    s = jnp.einsum('bqhd,bsd->bqhs', qi.astype(jnp.float32), ki_all.astype(jnp.float32))
    score = jnp.einsum('bqh,bqhs->bqs', wi.astype(jnp.float32), jax.nn.relu(s)) * INDEX_SCALE
    allowed = (kpos[None, :] // CHUNK) <= (qpos[:, None] // CHUNK)
    score = jnp.where(allowed[None], score, NEG)
    _, idx = lax.top_k(score, topk)
    kg = jax.vmap(lambda a, i: a[i])(k_all, idx)
    vg = jax.vmap(lambda a, i: a[i])(v_all, idx)
    valid = (kpos[idx] // CHUNK) <= (qpos[None, :, None] // CHUNK)
    qg = q.reshape(B, Tq, N_KV_HEADS, N_HEADS // N_KV_HEADS, HEAD_DIM)
    logits = jnp.einsum('bqgrd,bqkgd->bqgrk', qg, kg).astype(jnp.float32) * (HEAD_DIM ** -0.5)
    logits = jnp.where(valid[:, :, None, None, :], logits, NEG)
    p = jax.nn.softmax(logits, axis=-1).astype(v_all.dtype)
    o = jnp.einsum('bqgrk,bqkgd->bqgrd', p, vg)
    return o.reshape(B, Tq, N_HEADS * HEAD_DIM)


def _sgu(u, vn, w_s, b_s):
    B, T, _ = u.shape
    cl = min(T, MLP_CHUNK)
    n = T // cl
    mask = jnp.tril(jnp.ones((cl, cl), dtype=w_s.dtype))
    ws = w_s[:, :cl, :cl] * mask[None]
    vr = vn.reshape(B, n, cl, N_SGU_GROUPS, SGU_GROUP_DIM)
    mixed = jnp.einsum('gts,bnsgc->bntgc', ws, vr) + b_s[:, :cl].T[None, None, :, :, None]
    return u * mixed.reshape(B, T, SGU_WIDTH)


def _layer(x, c, pos, past, w_ada, b_ada, g_pre_mix, g_post_mix, g_pre_ffn, g_post_ffn,
           w_in, b_gate, g_sgu, w_spatial, b_spatial, w_proj_a, w_proj_b, w_out,
           w_ffn_in, w_ffn_out):
    B, T, _ = x.shape
    mod = jax.nn.silu(c) @ w_ada + b_ada
    sh1, sc1, gt1, sh2, sc2, gt2 = [m[:, None, :] for m in jnp.split(mod, 6, axis=-1)]

    xn = _rmsnorm(x, g_pre_mix) * (1.0 + sc1) + sh1
    proj = xn @ w_in
    q, k, v, qi, ki, wi, u, vb, gates = jnp.split(proj, SPLIT_POINTS, axis=-1)
    q = _rope(q.reshape(B, T, N_HEADS, HEAD_DIM), pos)
    k = _rope(k.reshape(B, T, N_KV_HEADS, HEAD_DIM), pos)
    v = v.reshape(B, T, N_KV_HEADS, HEAD_DIM)
    qi = _rope(qi.reshape(B, T, N_IDX_HEADS, IDX_DIM), pos)
    ki = _rope(ki[:, :, None, :], pos)[:, :, 0, :]

    if past is None:
        k_all, v_all, ki_all, kpos = k, v, ki, pos
    else:
        pk, pv, pki = past
        k_all = jnp.concatenate([pk, k], axis=1)
        v_all = jnp.concatenate([pv, v], axis=1)
        ki_all = jnp.concatenate([pki, ki], axis=1)
        kpos = jnp.arange(k_all.shape[1], dtype=jnp.int32)
    L = k_all.shape[1]
    topk = min(TOPK_MAX, L // 4)

    if T > Q_BLOCK and T % Q_BLOCK == 0:
        nb = T // Q_BLOCK
        to_blocks = lambda a: jnp.swapaxes(a.reshape((B, nb, Q_BLOCK) + a.shape[2:]), 0, 1)
        def blk(args):
            qb, qib, wib, pb = args
            return _dsa_block(qb, qib, wib, pb, k_all, v_all, ki_all, kpos, topk)
        att = lax.map(blk, (to_blocks(q), to_blocks(qi), to_blocks(wi), pos.reshape(nb, Q_BLOCK)))
        att = jnp.swapaxes(att, 0, 1).reshape(B, T, N_HEADS * HEAD_DIM)
    else:
        att = _dsa_block(q, qi, wi, pos, k_all, v_all, ki_all, kpos, topk)

    u = jax.nn.gelu(u)
    vn = _layernorm(jax.nn.gelu(vb), g_sgu)
    sgu = _sgu(u, vn, w_spatial, b_spatial)

    ga, gb = jnp.split(jax.nn.sigmoid(gates + b_gate), 2, axis=-1)
    merged = ga * (att @ w_proj_a) + gb * (sgu @ w_proj_b)
    h = merged @ w_out
    x = x + gt1 * _rmsnorm(h, g_post_mix)

    xn2 = _rmsnorm(x, g_pre_ffn) * (1.0 + sc2) + sh2
    gg, up = jnp.split(xn2 @ w_ffn_in, 2, axis=-1)
    h2 = (jax.nn.silu(gg) * up) @ w_ffn_out
    x = x + gt2 * _rmsnorm(h2, g_post_ffn)
    return x, k, v, ki, vn


def setup_inputs(seed: int = 0) -> dict:
    key = jax.random.key(seed)
    ks = jax.random.split(key, 32)
    nrm = lambda k, shape, s: jax.random.normal(k, shape, jnp.float32) * s
    gain = lambda k, shape: 1.0 + 0.01 * jax.random.normal(k, shape, jnp.float32)
    HW = N_HEADS * HEAD_DIM
    return {
        'x_prompt': nrm(ks[0], (BATCH, SEQ, D_MODEL), 1.0),
        'x_sample': nrm(ks[1], (DEC_BATCH, DEC_SEQ, D_MODEL), 1.0),
        'cache_k': nrm(ks[2], (DEPTH, DEC_BATCH, PAST_LEN, N_KV_HEADS, HEAD_DIM), 1.0),
        'cache_v': nrm(ks[3], (DEPTH, DEC_BATCH, PAST_LEN, N_KV_HEADS, HEAD_DIM), 1.0),
        'cache_idx_k': nrm(ks[4], (DEPTH, DEC_BATCH, PAST_LEN, IDX_DIM), 1.0),
        'c_prompt': nrm(ks[5], (BATCH, D_MODEL), 1.0),
        'c_sample': nrm(ks[6], (DEC_BATCH, D_MODEL), 1.0),
        'w_ada': nrm(ks[7], (DEPTH, D_MODEL, 6 * D_MODEL), 0.5 * D_MODEL ** -0.5),
        'b_ada': nrm(ks[8], (DEPTH, 6 * D_MODEL), 0.01),
        'g_pre_mix': gain(ks[9], (DEPTH, D_MODEL)),
        'g_post_mix': gain(ks[10], (DEPTH, D_MODEL)),
        'g_pre_ffn': gain(ks[11], (DEPTH, D_MODEL)),
        'g_post_ffn': gain(ks[12], (DEPTH, D_MODEL)),
        'w_in': nrm(ks[13], (DEPTH, D_MODEL, N_IN), D_MODEL ** -0.5),
        'b_gate': nrm(ks[14], (DEPTH, 2 * D_MODEL), 0.01),
        'g_sgu': gain(ks[15], (DEPTH, SGU_WIDTH)),
        'w_spatial': nrm(ks[16], (DEPTH, N_SGU_GROUPS, MLP_CHUNK, MLP_CHUNK), MLP_CHUNK ** -0.5),
        'b_spatial': gain(ks[17], (DEPTH, N_SGU_GROUPS, MLP_CHUNK)),
        'w_proj_a': nrm(ks[18], (DEPTH, HW, D_MODEL), HW ** -0.5),
        'w_proj_b': nrm(ks[19], (DEPTH, SGU_WIDTH, D_MODEL), SGU_WIDTH ** -0.5),
        'w_out': nrm(ks[20], (DEPTH, D_MODEL, D_MODEL), D_MODEL ** -0.5),
        'w_ffn_in': nrm(ks[21], (DEPTH, D_MODEL, 2 * FFN_HIDDEN), D_MODEL ** -0.5),
        'w_ffn_out': nrm(ks[22], (DEPTH, FFN_HIDDEN, D_MODEL), FFN_HIDDEN ** -0.5),
    }


def reference(x_prompt, x_sample, cache_k, cache_v, cache_idx_k, c_prompt, c_sample,
              w_ada, b_ada, g_pre_mix, g_post_mix, g_pre_ffn, g_post_ffn, w_in, b_gate,
              g_sgu, w_spatial, b_spatial, w_proj_a, w_proj_b, w_out, w_ffn_in, w_ffn_out):
    t_p = x_prompt.shape[1]
    t_s = x_sample.shape[1]
    past_len = cache_k.shape[2]
    pos_p = jnp.arange(t_p, dtype=jnp.int32)
    pos_s = past_len + jnp.arange(t_s, dtype=jnp.int32)

    xp, xs = x_prompt, x_sample
    kp_l, vp_l, kip_l, ks_l, vs_l, kis_l, vn_l = [], [], [], [], [], [], []
    for l in range(DEPTH):
        wl = (w_ada[l], b_ada[l], g_pre_mix[l], g_post_mix[l], g_pre_ffn[l], g_post_ffn[l],
              w_in[l], b_gate[l], g_sgu[l], w_spatial[l], b_spatial[l], w_proj_a[l],
              w_proj_b[l], w_out[l], w_ffn_in[l], w_ffn_out[l])
        xp, kp, vp, kip, _ = _layer(xp, c_prompt, pos_p, None, *wl)
        xs, ks, vs, kis, vns = _layer(xs, c_sample, pos_s,
                                      (cache_k[l], cache_v[l], cache_idx_k[l]), *wl)
        kp_l.append(kp); vp_l.append(vp); kip_l.append(kip)
        ks_l.append(ks); vs_l.append(vs); kis_l.append(kis); vn_l.append(vns)

    return (xp, xs, jnp.stack(kp_l), jnp.stack(vp_l), jnp.stack(kip_l),
            jnp.stack(ks_l), jnp.stack(vs_l), jnp.stack(kis_l), jnp.stack(vn_l))
```

```python
import functools

import jax
import jax.numpy as jnp
import numpy as np
from jax import lax
from jax.experimental import pallas as pl
from jax.experimental.pallas import tpu as pltpu

F32 = jnp.float32
BF16 = jnp.bfloat16

CHUNK = 64
N_HEADS = 8
N_KV_HEADS = 2
HEAD_DIM = 128
N_IDX_HEADS = 8
IDX_DIM = 64
TOPK_MAX = 256
ROPE_THETA = 500000.0
ROT_FRACTION = 4
MLP_CHUNK = 128
N_SGU_GROUPS = 8
NEG = -1e30
BIG = 3e38
EPS = 1e-6

LANES = 128
VMEM_LIMIT_BYTES = 56 * 1024 * 1024

HEADS_PER_KV = N_HEADS // N_KV_HEADS
IDX_COLS = N_IDX_HEADS * IDX_DIM
IDX_PAD = IDX_COLS + LANES
WI_LANE = IDX_DIM
BISECT_CAP = 40


def _cparams(n_axes):
    return pltpu.CompilerParams(
        dimension_semantics=("arbitrary",) * n_axes,
        vmem_limit_bytes=VMEM_LIMIT_BYTES,
    )


def _const_spec(shape):
    nd = len(shape)
    return pl.BlockSpec(shape, lambda *_: (0,) * nd, pipeline_mode=pl.Buffered(1))


def _split_bf16(x):
    hi = x.astype(BF16)
    lo = (x - hi.astype(F32)).astype(BF16)
    return hi, lo


def _gelu(x):
    c = np.float32(np.sqrt(2.0 / np.pi))
    return 0.5 * x * (1.0 + jnp.tanh(c * (x + 0.044715 * (x * x * x))))


def _rms(x, g):
    return x * lax.rsqrt(jnp.mean(x * x, axis=-1, keepdims=True) + EPS) * g


def _ada_kernel(c_ref, w_ref, b_ref, o_ref):
    c = c_ref[...]
    a = c * jax.nn.sigmoid(c)
    ah, al = _split_bf16(a)
    wh, wl = _split_bf16(w_ref[...])
    dot = functools.partial(jnp.dot, preferred_element_type=F32)
    o_ref[...] = dot(ah, wh) + dot(al, wh) + dot(ah, wl) + b_ref[...]


def _ada(c, w_ada, b_ada):
    rows, d = c.shape
    n = w_ada.shape[1]
    tn = 1024
    return pl.pallas_call(
        _ada_kernel,
        grid=(n // tn,),
        in_specs=[
            pl.BlockSpec((rows, d), lambda j: (0, 0)),
            pl.BlockSpec((d, tn), lambda j: (0, j)),
            pl.BlockSpec((1, tn), lambda j: (0, j)),
        ],
        out_specs=pl.BlockSpec((rows, tn), lambda j: (0, j)),
        out_shape=jax.ShapeDtypeStruct((rows, n), F32),
        compiler_params=_cparams(1),
        name="ada",
    )(c, w_ada, b_ada.reshape(1, n))


def _rope_tables(pos, head_dim):
    rd = head_dim // ROT_FRACTION
    half = rd // 2
    inv = jnp.power(ROPE_THETA, -(jnp.arange(half, dtype=F32) / half))
    ang = pos.astype(F32)[:, None] * inv[None, :]
    cos, sin = jnp.cos(ang), jnp.sin(ang)
    t = pos.shape[0]
    ones = jnp.ones((t, head_dim - rd), F32)
    zeros = jnp.zeros((t, head_dim - rd), F32)
    c = jnp.concatenate([cos, cos, ones], axis=-1)
    s = jnp.concatenate([-sin, sin, zeros], axis=-1)
    reps = LANES // head_dim
    c, s = jnp.tile(c, (1, reps)), jnp.tile(s, (1, reps))
    return jnp.stack([c, s])


def _rope_apply(x, tab_ref, half):
    lane = lax.broadcasted_iota(jnp.int32, x.shape, 1)
    fwd = pltpu.roll(x, LANES - half, 1)
    bwd = pltpu.roll(x, half, 1)
    partner = jnp.where((lane % (2 * half)) < half, fwd, bwd)
    return x * tab_ref[0] + partner * tab_ref[1]


def _proj_kernel(x_ref, sh_ref, sc_ref, gpre_ref, wmain_ref, wih_ref, wil_ref,
                 tq_ref, ti_ref, tk_ref, bgate_ref, gsgu_ref,
                 q_ref, k_ref, v_ref, kbf_ref, vbf_ref, qi_ref, kw_ref, ki4_ref,
                 u_ref, vn_ref, gates_ref):
    nb, t, d = x_ref.shape
    rows = nb * t
    xn = _rms(x_ref[...], gpre_ref[...]) * (1.0 + sc_ref[...]) + sh_ref[...]
    xn = xn.reshape(rows, d)
    xh, xl = _split_bf16(xn)
    dot = functools.partial(jnp.dot, preferred_element_type=F32)
    q_scale = np.float32(HEAD_DIM ** -0.5)
    qw = N_HEADS * HEAD_DIM
    kvw = N_KV_HEADS * HEAD_DIM
    half_h = HEAD_DIM // ROT_FRACTION // 2
    half_i = IDX_DIM // ROT_FRACTION // 2

    y = dot(xh, wmain_ref[:, 0:qw])
    for h in range(N_HEADS):
        sl = slice(h * HEAD_DIM, (h + 1) * HEAD_DIM)
        q_ref[:, sl] = (_rope_apply(y[:, sl], tq_ref, half_h) * q_scale).astype(BF16)

    y = dot(xh, wmain_ref[:, qw:qw + kvw])
    for h in range(N_KV_HEADS):
        sl = slice(h * HEAD_DIM, (h + 1) * HEAD_DIM)
        kr = _rope_apply(y[:, sl], tq_ref, half_h)
        k_ref[:, sl] = kr
        kbf_ref[:, sl] = kr.astype(BF16)
    y = dot(xh, wmain_ref[:, qw + kvw:qw + 2 * kvw])
    v_ref[...] = y
    vbf_ref[...] = y.astype(BF16)

    wih = wih_ref[...]
    yi = dot(xh, wih) + dot(xl, wih) + dot(xh, wil_ref[...])
    for tile in range(IDX_COLS // LANES):
        sl = slice(tile * LANES, (tile + 1) * LANES)
        qi_ref[:, sl] = _rope_apply(yi[:, sl], ti_ref, half_i)
    kw = _rope_apply(yi[:, IDX_COLS:IDX_PAD], tk_ref, half_i)
    kw_ref[...] = kw
    lane = lax.broadcasted_iota(jnp.int32, kw.shape, 1)
    kk = jnp.where(lane < IDX_DIM, kw, pltpu.roll(kw, IDX_DIM, 1))
    kh, kl = _split_bf16(kk)
    ki4_ref[:, 0:LANES] = kh
    ki4_ref[:, LANES:2 * LANES] = kl

    base = qw + 2 * kvw
    u_ref[...] = _gelu(dot(xh, wmain_ref[:, base:base + d])).astype(u_ref.dtype)
    g = _gelu(dot(xh, wmain_ref[:, base + d:base + 2 * d]))
    mu = jnp.mean(g, axis=-1, keepdims=True)
    gc = g - mu
    vn = gc * lax.rsqrt(jnp.mean(gc * gc, axis=-1, keepdims=True) + EPS) * gsgu_ref[...]
    vn_ref[...] = vn.astype(vn_ref.dtype)

    y = dot(xh, wmain_ref[:, base + 2 * d:base + 4 * d]) + bgate_ref[...]
    gates_ref[...] = jax.nn.sigmoid(y).astype(BF16)


def _proj(x, sh, sc, gpre, wmain, wih, wil, tabs, tab_map, bgate, gsgu, nb, vn_dtype):
    s_total, t_full, d = x.shape
    tq, ti, tk = tabs
    if nb == 1:
        t = min(t_full, 512)
        grid = (s_total, t_full // t)
        xmap = lambda b, j: (b, j, 0)
        mmap = lambda b, j: (b, 0, 0)
        rmap = lambda b, j: (b * (t_full // t) + j, 0)
        tmap = lambda b, j: (0,) + tab_map(j)
    else:
        t = t_full
        grid = (s_total // nb,)
        xmap = lambda b: (b, 0, 0)
        mmap = lambda b: (b, 0, 0)
        rmap = lambda b: (b, 0)
        tmap = lambda b: (0,) + tab_map(0)
    rows = nb * t
    n_rows = s_total * t_full
    qw, kvw = N_HEADS * HEAD_DIM, N_KV_HEADS * HEAD_DIM

    def row_spec(width):
        return pl.BlockSpec((rows, width), rmap)

    def row_shape(width, dtype):
        return jax.ShapeDtypeStruct((n_rows, width), dtype)

    tab_spec = pl.BlockSpec((2, rows, LANES), tmap)
    outs = [(qw, BF16), (kvw, F32), (kvw, F32), (kvw, BF16), (kvw, BF16), (IDX_COLS, F32),
            (LANES, F32), (2 * LANES, BF16), (d, BF16), (d, vn_dtype), (2 * d, BF16)]
    return pl.pallas_call(
        _proj_kernel,
        grid=grid,
        in_specs=[
            pl.BlockSpec((nb, t, d), xmap),
            pl.BlockSpec((nb, 1, d), mmap),
            pl.BlockSpec((nb, 1, d), mmap),
            _const_spec((1, 1, d)),
            _const_spec(wmain.shape),
            _const_spec(wih.shape),
            _const_spec(wil.shape),
            tab_spec, tab_spec, tab_spec,
            _const_spec((1, 2 * d)),
            _const_spec((1, d)),
        ],
        out_specs=[row_spec(w) for w, _ in outs],
        out_shape=[row_shape(w, dt) for w, dt in outs],
        compiler_params=_cparams(len(grid)),
        name="proj",
    )(x, sh, sc, gpre.reshape(1, 1, d), wmain, wih, wil, tq, ti, tk,
      bgate.reshape(1, 2 * d), gsgu.reshape(1, d))


def _dsa_kernel(q_ref, qi_ref, kw_ref, ki4_ref, k_ref, v_ref, o_ref,
                scr, qi4_scr, wb_scr, qg_scr, acc_scr, m_scr, l_scr, thr_scr, lo_scr, hi_scr,
                tie_scr, done_scr, *, tq, kb, topk, causal, n_keys):
    qb = pl.program_id(1)
    row = lax.broadcasted_iota(jnp.int32, (tq, 1), 0)
    if causal:
        n_adm = ((qb * tq + row) // CHUNK + 1) * CHUNK
        nkb = ((qb + 1) * tq + kb - 1) // kb
    else:
        n_adm = jnp.full((tq, 1), n_keys, jnp.int32)
        nkb = (n_keys + kb - 1) // kb
    lane128 = lax.broadcasted_iota(jnp.int32, (tq, LANES), 1)
    nt = (((1,), (1,)), ((), ()))

    qi = qi_ref[0]
    for h in range(N_IDX_HEADS):
        tile = qi[:, (h // 2) * LANES:(h // 2 + 1) * LANES]
        own = (lane128 < IDX_DIM) if h % 2 == 0 else (lane128 >= IDX_DIM)
        both = jnp.where(own, tile, pltpu.roll(tile, IDX_DIM, 1))
        hi = both.astype(BF16).astype(F32)
        mix = jnp.where(lane128 < IDX_DIM, hi, both - hi).astype(BF16)
        qi4_scr[h * tq:(h + 1) * tq, 0:LANES] = mix
        qi4_scr[h * tq:(h + 1) * tq, LANES:2 * LANES] = mix
        wb_scr[h] = jnp.broadcast_to(kw_ref[0, :, WI_LANE + h:WI_LANE + h + 1], (tq, LANES))
    for g in range(N_KV_HEADS):
        for r in range(HEADS_PER_KV):
            hh = g * HEADS_PER_KV + r
            qg_scr[g, r * tq:(r + 1) * tq, :] = q_ref[0, :, hh * HEAD_DIM:(hh + 1) * HEAD_DIM]

    def score_block(j, carry):
        rmin, rmax = carry
        kblk = ki4_ref[0, pl.ds(pl.multiple_of(j * kb, kb), kb), :]
        s_all = lax.dot_general(qi4_scr[...], kblk, nt, preferred_element_type=F32)
        for c in range(kb // LANES):
            cs = slice(c * LANES, (c + 1) * LANES)
            acc = jnp.zeros((tq, LANES), F32)
            for h in range(N_IDX_HEADS):
                acc = acc + wb_scr[h] * jnp.maximum(s_all[h * tq:(h + 1) * tq, cs], 0.0)
            kpos = j * kb + c * LANES + lane128
            allowed = kpos < n_adm
            scr[j, :, cs] = jnp.where(allowed, acc, NEG)
            rmin = jnp.minimum(rmin, jnp.where(allowed, acc, BIG))
            rmax = jnp.maximum(rmax, jnp.where(allowed, acc, -BIG))
        return rmin, rmax

    rmin, rmax = lax.fori_loop(
        0, nkb, score_block,
        (jnp.full((tq, LANES), BIG, F32), jnp.full((tq, LANES), -BIG, F32)))
    rmin = jnp.min(rmin, axis=1, keepdims=True)
    rmax = jnp.max(rmax, axis=1, keepdims=True)

    def count_ge(t):
        tb = jnp.broadcast_to(t, (tq, LANES))

        def body(j, acc):
            for c in range(kb // LANES):
                acc = acc + jnp.where(scr[j, :, c * LANES:(c + 1) * LANES] >= tb, 1.0, 0.0)
            return acc

        acc = lax.fori_loop(0, nkb, body, jnp.zeros((tq, LANES), F32))
        return jnp.sum(acc, axis=1, keepdims=True)

    kf = np.float32(topk)
    need_search = n_adm > topk
    searchable = jnp.logical_and(need_search, rmin < rmax)
    thr_scr[...] = rmin
    lo_scr[...] = rmin
    hi_scr[...] = rmax
    tie_scr[...] = jnp.where(jnp.logical_and(need_search, jnp.logical_not(searchable)), 1.0, 0.0)
    done_scr[...] = jnp.where(searchable, 0.0, 1.0)

    def bis_cond(carry):
        it, n_open = carry
        return jnp.logical_and(it < BISECT_CAP, n_open > 0.0)

    def bis_body(carry):
        it, _ = carry
        lo, hi = lo_scr[...], hi_scr[...]
        is_open = done_scr[...] < 0.5
        mid = 0.5 * lo + 0.5 * hi
        cnt = count_ge(mid)
        hit = jnp.logical_and(is_open, cnt == kf)
        stuck = jnp.logical_or(mid <= lo, mid >= hi)
        stall = jnp.logical_and(is_open, jnp.logical_and(stuck, jnp.logical_not(hit)))
        move = jnp.logical_and(is_open, jnp.logical_not(jnp.logical_or(hit, stall)))
        ge = cnt >= kf
        thr_scr[...] = jnp.where(hit, mid, thr_scr[...])
        lo_scr[...] = jnp.where(jnp.logical_and(move, ge), mid, lo)
        hi_scr[...] = jnp.where(jnp.logical_and(move, jnp.logical_not(ge)), mid, hi)
        tie_scr[...] = jnp.where(stall, 1.0, tie_scr[...])
        done_scr[...] = jnp.where(jnp.logical_or(hit, stall), 1.0, done_scr[...])
        return it + 1, jnp.sum(jnp.where(move, 1.0, 0.0))

    n_open0 = jnp.sum(jnp.where(searchable, 1.0, 0.0))
    lax.while_loop(bis_cond, bis_body, (jnp.int32(0), n_open0))
    tie_scr[...] = jnp.where(done_scr[...] < 0.5, 1.0, tie_scr[...])

    @pl.when(jnp.sum(tie_scr[...]) > 0.0)
    def _ties():
        tie = tie_scr[...] > 0.5

        def max_below(bound, strict):
            bb = jnp.broadcast_to(bound, (tq, LANES))

            def body(j, acc):
                for c in range(kb // LANES):
                    s = scr[j, :, c * LANES:(c + 1) * LANES]
                    ok = (s < bb) if strict else (s <= bb)
                    acc = jnp.maximum(acc, jnp.where(ok, s, -BIG))
                return acc

            acc = lax.fori_loop(0, nkb, body, jnp.full((tq, LANES), -BIG, F32))
            return jnp.max(acc, axis=1, keepdims=True)

        def short(c):
            return jnp.logical_and(tie, c < kf)

        def snap_cond(carry):
            return jnp.sum(jnp.where(short(carry[1]), 1.0, 0.0)) > 0.0

        def snap_body(carry):
            t_old, c_old = carry
            t_new = max_below(t_old, True)
            c_new = count_ge(t_new)
            upd = short(c_old)
            return jnp.where(upd, t_new, t_old), jnp.where(upd, c_new, c_old)

        t0 = max_below(hi_scr[...], False)
        t, _ = lax.while_loop(snap_cond, snap_body, (t0, count_ge(t0)))
        tb = jnp.broadcast_to(t, (tq, LANES))

        def count_gt(j, acc):
            for c in range(kb // LANES):
                acc = acc + jnp.where(scr[j, :, c * LANES:(c + 1) * LANES] > tb, 1.0, 0.0)
            return acc

        n_gt = jnp.sum(lax.fori_loop(0, nkb, count_gt, jnp.zeros((tq, LANES), F32)),
                       axis=1, keepdims=True)
        need = kf - n_gt

        def count_eq_before(cut):
            cb = jnp.broadcast_to(cut, (tq, LANES))

            def body(j, acc):
                for c in range(kb // LANES):
                    s = scr[j, :, c * LANES:(c + 1) * LANES]
                    kpos = (j * kb + c * LANES + lane128).astype(F32)
                    acc = acc + jnp.where(s == tb, jnp.where(kpos < cb, 1.0, 0.0), 0.0)
                return acc

            return jnp.sum(lax.fori_loop(0, nkb, body, jnp.zeros((tq, LANES), F32)),
                           axis=1, keepdims=True)

        def cut_body(_, carry):
            clo, chi = carry
            cmid = jnp.floor(0.5 * (clo + chi))
            ok = count_eq_before(cmid) >= need
            return jnp.where(ok, clo, cmid), jnp.where(ok, cmid, chi)

        n_total = np.float32(scr.shape[0] * kb)
        steps = int(np.ceil(np.log2(scr.shape[0] * kb))) + 1
        _, cut = lax.fori_loop(0, steps, cut_body,
                               (jnp.zeros((tq, 1), F32), jnp.full((tq, 1), n_total, F32)))
        cutb = jnp.broadcast_to(jnp.where(tie, cut, n_total), (tq, LANES))

        def drop(j, carry):
            for c in range(kb // LANES):
                cs = slice(c * LANES, (c + 1) * LANES)
                s = scr[j, :, cs]
                kpos = (j * kb + c * LANES + lane128).astype(F32)
                dropped = jnp.where(s == tb, jnp.where(kpos >= cutb, 1.0, 0.0), 0.0)
                scr[j, :, cs] = jnp.where(dropped > 0.5, NEG, s)
            return carry

        lax.fori_loop(0, nkb, drop, 0)
        thr_scr[...] = jnp.where(tie, t, thr_scr[...])

    m_scr[...] = jnp.full(m_scr.shape, NEG, F32)
    l_scr[...] = jnp.zeros(l_scr.shape, F32)
    acc_scr[...] = jnp.zeros(acc_scr.shape, F32)
    thr = thr_scr[...]

    def attend(j, _):
        sel = scr[j] >= thr
        bias = jnp.where(sel, 0.0, NEG)
        bias = jnp.concatenate([bias] * HEADS_PER_KV, axis=0)
        koff = pl.ds(pl.multiple_of(j * kb, kb), kb)
        for g in range(N_KV_HEADS):
            kg = k_ref[0, koff, g * HEAD_DIM:(g + 1) * HEAD_DIM]
            vg = v_ref[0, koff, g * HEAD_DIM:(g + 1) * HEAD_DIM]
            lg = lax.dot_general(qg_scr[g], kg, nt, preferred_element_type=F32) + bias
            m_old = m_scr[g]
            m_new = jnp.maximum(m_old, jnp.max(lg, axis=1, keepdims=True))
            p = jnp.exp(lg - m_new)
            corr = jnp.exp(m_old - m_new)
            l_scr[g] = corr * l_scr[g] + jnp.sum(p, axis=1, keepdims=True)
            acc_scr[g] = corr * acc_scr[g] + jnp.dot(p.astype(BF16), vg,
                                                     preferred_element_type=F32)
            m_scr[g] = m_new
        return 0

    lax.fori_loop(0, nkb, attend, 0)
    for g in range(N_KV_HEADS):
        o = acc_scr[g] / l_scr[g]
        for r in range(HEADS_PER_KV):
            hh = g * HEADS_PER_KV + r
            o_ref[0, :, hh * HEAD_DIM:(hh + 1) * HEAD_DIM] = o[r * tq:(r + 1) * tq].astype(o_ref.dtype)


def _dsa(q, qi, kw, ki4, k_bf, v_bf, *, tq, kb, causal, n_keys):
    s_total, t_q, qw = q.shape
    lp = ki4.shape[1]
    assert lp % kb == 0 and t_q % tq == 0 and kb % LANES == 0
    topk = min(TOPK_MAX, n_keys // 4)
    kernel = functools.partial(_dsa_kernel, tq=tq, kb=kb, topk=topk, causal=causal, n_keys=n_keys)
    qmap = lambda b, j: (b, j, 0)
    kmap = lambda b, j: (b, 0, 0)
    rows_g = HEADS_PER_KV * tq
    return pl.pallas_call(
        kernel,
        grid=(s_total, t_q // tq),
        in_specs=[
            pl.BlockSpec((1, tq, qw), qmap),
            pl.BlockSpec((1, tq, IDX_COLS), qmap),
            pl.BlockSpec((1, tq, LANES), qmap),
            pl.BlockSpec((1, lp, 2 * LANES), kmap),
            pl.BlockSpec((1, lp, k_bf.shape[2]), kmap),
            pl.BlockSpec((1, lp, v_bf.shape[2]), kmap),
        ],
        out_specs=pl.BlockSpec((1, tq, qw), qmap),
        out_shape=jax.ShapeDtypeStruct((s_total, t_q, qw), BF16),
        scratch_shapes=[
            pltpu.VMEM((lp // kb, tq, kb), F32),
            pltpu.VMEM((N_IDX_HEADS * tq, 2 * LANES), BF16),
            pltpu.VMEM((N_IDX_HEADS, tq, LANES), F32),
            pltpu.VMEM((N_KV_HEADS, rows_g, HEAD_DIM), BF16),
            pltpu.VMEM((N_KV_HEADS, rows_g, HEAD_DIM), F32),
            pltpu.VMEM((N_KV_HEADS, rows_g, 1), F32),
            pltpu.VMEM((N_KV_HEADS, rows_g, 1), F32),
            pltpu.VMEM((tq, 1), F32),
            pltpu.VMEM((tq, 1), F32),
            pltpu.VMEM((tq, 1), F32),
            pltpu.VMEM((tq, 1), F32),
            pltpu.VMEM((tq, 1), F32),
        ],
        compiler_params=_cparams(2),
        name="dsa",
    )(q, qi, kw, ki4, k_bf, v_bf)


def _mix_kernel(x_ref, gt_ref, gpost_ref, att_ref, u_ref, vn_ref, gates_ref, ws_ref, bs_ref,
                wpa_ref, wpb_ref, wout_ref, o_ref, sgu_scr):
    nb, t, d = x_ref.shape
    rows = nb * t
    gd = d // N_SGU_GROUPS
    dot = functools.partial(jnp.dot, preferred_element_type=F32)
    r = lax.broadcasted_iota(jnp.int32, (MLP_CHUNK, MLP_CHUNK), 0)
    c = lax.broadcasted_iota(jnp.int32, (MLP_CHUNK, MLP_CHUNK), 1)
    for g in range(N_SGU_GROUPS):
        wg = jnp.where(r >= c, ws_ref[g], 0.0).astype(BF16)
        gs = slice(g * gd, (g + 1) * gd)
        for ch in range(rows // MLP_CHUNK):
            rs = slice(ch * MLP_CHUNK, (ch + 1) * MLP_CHUNK)
            mixed = dot(wg, vn_ref[rs, gs].astype(BF16)) + bs_ref[:, gs]
            sgu_scr[rs, gs] = (u_ref[rs, gs].astype(F32) * mixed).astype(BF16)
    a = dot(att_ref[...], wpa_ref[...])
    b = dot(sgu_scr[...], wpb_ref[...])
    merged = gates_ref[:, 0:d].astype(F32) * a + gates_ref[:, d:2 * d].astype(F32) * b
    h = dot(merged.astype(BF16), wout_ref[...])
    hn = _rms(h, gpost_ref[...])
    o_ref[...] = x_ref[...] + gt_ref[...] * hn.reshape(nb, t, d)


def _row_blocking(s_total, t_full, nb):
    if nb == 1:
        t = min(t_full, 512)
        grid = (s_total, t_full // t)
        xmap = lambda b, j: (b, j, 0)
        mmap = lambda b, j: (b, 0, 0)
        rmap = lambda b, j: (b * (t_full // t) + j, 0)
    else:
        t = t_full
        grid = (s_total // nb,)
        xmap = lambda b: (b, 0, 0)
        mmap = lambda b: (b, 0, 0)
        rmap = lambda b: (b, 0)
    return t, grid, xmap, mmap, rmap


def _mix(x, gt, gpost, att, u, vn, gates, ws, bs, wpa, wpb, wout, nb):
    s_total, t_full, d = x.shape
    t, grid, xmap, mmap, rmap = _row_blocking(s_total, t_full, nb)
    rows = nb * t
    return pl.pallas_call(
        _mix_kernel,
        grid=grid,
        in_specs=[
            pl.BlockSpec((nb, t, d), xmap),
            pl.BlockSpec((nb, 1, d), mmap),
            _const_spec((1, 1, d)),
            pl.BlockSpec((rows, d), rmap),
            pl.BlockSpec((rows, d), rmap),
            pl.BlockSpec((rows, d), rmap),
            pl.BlockSpec((rows, 2 * d), rmap),
            _const_spec(ws.shape),
            _const_spec(bs.shape),
            _const_spec(wpa.shape),
            _const_spec(wpb.shape),
            _const_spec(wout.shape),
        ],
        out_specs=pl.BlockSpec((nb, t, d), xmap),
        out_shape=jax.ShapeDtypeStruct(x.shape, F32),
        scratch_shapes=[pltpu.VMEM((rows, d), BF16)],
        compiler_params=_cparams(len(grid)),
        name="mix",
    )(x, gt, gpost.reshape(1, 1, d), att, u, vn, gates, ws, bs, wpa, wpb, wout)


def _ffn_kernel(x_ref, sh_ref, sc_ref, gt_ref, gpre_ref, gpost_ref, win_ref, wout_ref, o_ref,
                *, n_split):
    nb, t, d = x_ref.shape
    rows = nb * t
    hidden = wout_ref.shape[0]
    hc = hidden // n_split
    dot = functools.partial(jnp.dot, preferred_element_type=F32)
    x = x_ref[...]
    xn = (_rms(x, gpre_ref[...]) * (1.0 + sc_ref[...]) + sh_ref[...]).reshape(rows, d).astype(BF16)
    acc = jnp.zeros((rows, d), F32)
    for i in range(n_split):
        gg = dot(xn, win_ref[:, i * hc:(i + 1) * hc])
        up = dot(xn, win_ref[:, hidden + i * hc:hidden + (i + 1) * hc])
        hmid = (gg * jax.nn.sigmoid(gg) * up).astype(BF16)
        acc = acc + dot(hmid, wout_ref[i * hc:(i + 1) * hc, :])
    hn = _rms(acc, gpost_ref[...])
    o_ref[...] = x + gt_ref[...] * hn.reshape(nb, t, d)


def _ffn(x, sh, sc, gt, gpre, gpost, win, wout, nb):
    s_total, t_full, d = x.shape
    t, grid, xmap, mmap, _ = _row_blocking(s_total, t_full, nb)
    hidden = wout.shape[0]
    n_split = 2
    assert hidden % (n_split * LANES) == 0
    return pl.pallas_call(
        functools.partial(_ffn_kernel, n_split=n_split),
        grid=grid,
        in_specs=[
            pl.BlockSpec((nb, t, d), xmap),
            pl.BlockSpec((nb, 1, d), mmap),
            pl.BlockSpec((nb, 1, d), mmap),
            pl.BlockSpec((nb, 1, d), mmap),
            _const_spec((1, 1, d)),
            _const_spec((1, 1, d)),
            _const_spec(win.shape),
            _const_spec(wout.shape),
        ],
        out_specs=pl.BlockSpec((nb, t, d), xmap),
        out_shape=jax.ShapeDtypeStruct(x.shape, F32),
        compiler_params=_cparams(len(grid)),
        name="ffn",
    )(x, sh, sc, gt, gpre.reshape(1, 1, d), gpost.reshape(1, 1, d), win, wout)


def _prep_weights(w_in, w_proj_a, w_proj_b, w_out, w_ffn_in, w_ffn_out, d):
    qw, kvw = N_HEADS * HEAD_DIM, N_KV_HEADS * HEAD_DIM
    o_q, o_k, o_v = 0, qw, qw + kvw
    o_qi = qw + 2 * kvw
    o_ki = o_qi + IDX_COLS
    o_wi = o_ki + IDX_DIM
    o_u = o_wi + N_IDX_HEADS
    wmain = jnp.concatenate([w_in[:, o_q:o_qi], w_in[:, o_u:]], axis=1).astype(BF16)
    pad = jnp.zeros((d, IDX_PAD - IDX_COLS - IDX_DIM - N_IDX_HEADS), F32)
    widx = jnp.concatenate([w_in[:, o_qi:o_u], pad], axis=1)
    wih, wil = _split_bf16(widx)
    return dict(
        wmain=wmain, wih=wih, wil=wil,
        wpa=w_proj_a.astype(BF16), wpb=w_proj_b.astype(BF16), wout=w_out.astype(BF16),
        wfi=w_ffn_in.astype(BF16), wfo=w_ffn_out.astype(BF16),
    )


def _spatial_tables(w_spatial, b_spatial, seq_len, d):
    cl = min(seq_len, MLP_CHUNK)
    reps = MLP_CHUNK // cl
    gd = d // N_SGU_GROUPS
    w = w_spatial[:, :cl, :cl]
    if reps > 1:
        eye = jnp.eye(reps, dtype=w.dtype)
        w = jnp.einsum("ab,gts->gatbs", eye, w).reshape(N_SGU_GROUPS, MLP_CHUNK, MLP_CHUNK)
    b = jnp.tile(b_spatial[:, :cl].T, (reps, 1))
    b = jnp.repeat(b, gd, axis=1)
    return w, b


def _stream(x, mod, pos, wts, g_pre_mix, g_post_mix, g_pre_ffn, g_post_ffn, b_gate, g_sgu,
            w_spatial, b_spatial, past, nb):
    s_total, t_full, d = x.shape
    sh1, sc1, gt1, sh2, sc2, gt2 = [m[:, None, :] for m in jnp.split(mod, 6, axis=-1)]

    idx_scale = np.float32((IDX_DIM * N_IDX_HEADS) ** -0.5)
    kw_scale = jnp.ones((LANES,), F32).at[WI_LANE:WI_LANE + N_IDX_HEADS].set(idx_scale)
    if nb == 1:
        tpos = pos
        tab_map = lambda j: (j, 0)
    else:
        tpos = jnp.tile(pos, nb)
        tab_map = lambda j: (0, 0)
    tabs = (_rope_tables(tpos, HEAD_DIM), _rope_tables(tpos, IDX_DIM), _kw_tables(tpos, kw_scale))

    vn_dtype = F32 if past is not None else BF16
    (q, k, v, k_bf, v_bf, qi, kw, ki4, u, vn, gates) = _proj(
        x, sh1, sc1, g_pre_mix, wts["wmain"], wts["wih"], wts["wil"], tabs, tab_map,
        b_gate, g_sgu, nb, vn_dtype)

    shp = lambda a: a.reshape(s_total, t_full, a.shape[-1])
    if past is None:
        n_keys = t_full
        att = _dsa(shp(q), shp(qi), shp(kw), shp(ki4), shp(k_bf), shp(v_bf),
                   tq=min(128, t_full), kb=min(512, t_full), causal=True, n_keys=n_keys)
    else:
        pk, pv, pki = past
        p_len = pk.shape[1]
        n_keys = p_len + t_full
        kb = LANES * ((n_keys + LANES - 1) // LANES)
        padn = kb - n_keys
        pkh, pkl = _split_bf16(pki)
        ki4_all = jnp.concatenate(
            [jnp.concatenate([pkh, pkh, pkl, pkl], axis=-1), shp(ki4)], axis=1)
        cat = lambda a, b_: jnp.pad(jnp.concatenate([a.astype(BF16), b_], axis=1),
                                    ((0, 0), (0, padn), (0, 0)))
        ki4_all = jnp.pad(ki4_all, ((0, 0), (0, padn), (0, 0)))
        att = _dsa(shp(q), shp(qi), shp(kw), ki4_all,
                   cat(pk.reshape(s_total, p_len, -1), shp(k_bf)),
                   cat(pv.reshape(s_total, p_len, -1), shp(v_bf)),
                   tq=t_full, kb=kb, causal=False, n_keys=n_keys)
    att = att.reshape(s_total * t_full, -1)

    ws, bs = _spatial_tables(w_spatial, b_spatial, t_full, d)
    x1 = _mix(x, gt1, g_post_mix, att, u, vn, gates, ws, bs,
              wts["wpa"], wts["wpb"], wts["wout"], nb)
    x2 = _ffn(x1, sh2, sc2, gt2, g_pre_ffn, g_post_ffn, wts["wfi"], wts["wfo"], nb)
    return x2, shp(k), shp(v), shp(kw)[..., :IDX_DIM], shp(vn)


def _kw_tables(pos, scale_lanes):
    tab = _rope_tables(pos, IDX_DIM)
    lane = jnp.arange(LANES)
    c = jnp.where(lane[None, :] < IDX_DIM, tab[0], 1.0) * scale_lanes[None, :]
    s = jnp.where(lane[None, :] < IDX_DIM, tab[1], 0.0)
    return jnp.stack([c, s])


def kernel(x_prompt, x_sample, cache_k, cache_v, cache_idx_k, c_prompt, c_sample, w_ada, b_ada,
           g_pre_mix, g_post_mix, g_pre_ffn, g_post_ffn, w_in, b_gate, g_sgu, w_spatial,
           b_spatial, w_proj_a, w_proj_b, w_out, w_ffn_in, w_ffn_out):
    depth = w_ada.shape[0]
    bp, t_p, d = x_prompt.shape
    bs_, t_s, _ = x_sample.shape
    past_len = cache_k.shape[2]
    pos_p = jnp.arange(t_p, dtype=jnp.int32)
    pos_s = past_len + jnp.arange(t_s, dtype=jnp.int32)
    nb_s = max(1, min(bs_, 256 // t_s))

    xp, xs = x_prompt, x_sample
    outs = [[] for _ in range(7)]
    for l in range(depth):
        wts = _prep_weights(w_in[l], w_proj_a[l], w_proj_b[l], w_out[l], w_ffn_in[l],
                            w_ffn_out[l], d)
        c_all = jnp.concatenate([c_prompt, c_sample], axis=0)
        pad_rows = (-c_all.shape[0]) % 8
        c_all = jnp.pad(c_all, ((0, pad_rows), (0, 0)))
        mod = _ada(c_all, w_ada[l], b_ada[l])
        common = (wts, g_pre_mix[l], g_post_mix[l], g_pre_ffn[l], g_post_ffn[l], b_gate[l],
                  g_sgu[l], w_spatial[l], b_spatial[l])
        xp, kp, vp, kip, _ = _stream(xp, mod[:bp], pos_p, *common, None, 1)
        past = (cache_k[l], cache_v[l], cache_idx_k[l])
        xs, ks, vs, kis, vns = _stream(xs, mod[bp:bp + bs_], pos_s, *common, past, nb_s)
        hd = (N_KV_HEADS, HEAD_DIM)
        for lst, val in zip(outs, (kp.reshape(bp, t_p, *hd), vp.reshape(bp, t_p, *hd), kip,
                                   ks.reshape(bs_, t_s, *hd), vs.reshape(bs_, t_s, *hd), kis,
                                   vns)):
            lst.append(val)
    return (xp, xs) + tuple(jnp.stack(o) for o in outs)
```

```python
import functools

import jax
import jax.numpy as jnp
import numpy as np
from jax import lax
from jax.experimental import pallas as pl
from jax.experimental.pallas import tpu as pltpu

F32 = jnp.float32
BF16 = jnp.bfloat16

CHUNK = 64
N_HEADS = 8
N_KV_HEADS = 2
HEAD_DIM = 128
N_IDX_HEADS = 8
IDX_DIM = 64
TOPK_MAX = 256
ROPE_THETA = 500000.0
ROT_FRACTION = 4
MLP_CHUNK = 128
N_SGU_GROUPS = 8
NEG = -1e30
BIG = 3e38
EPS = 1e-6

LANES = 128
VMEM_LIMIT_BYTES = 56 * 1024 * 1024

HEADS_PER_KV = N_HEADS // N_KV_HEADS
IDX_COLS = N_IDX_HEADS * IDX_DIM
IDX_PAD = IDX_COLS + LANES
WI_LANE = IDX_DIM
DSA_Q_BLOCK = 128
DSA_KEY_BLOCK = 512
ATT_ROW_TILE = 32
LOG2E = float(np.log2(np.e))
BISECT_CAP = 40


def _cparams(n_axes):
    return pltpu.CompilerParams(
        dimension_semantics=("arbitrary",) * n_axes,
        vmem_limit_bytes=VMEM_LIMIT_BYTES,
    )


def _const_spec(shape):
    nd = len(shape)
    return pl.BlockSpec(shape, lambda *_: (0,) * nd, pipeline_mode=pl.Buffered(1))


def _split_bf16(x):
    hi = x.astype(BF16)
    lo = (x - hi.astype(F32)).astype(BF16)
    return hi, lo


def _gelu(x):
    c = np.float32(np.sqrt(2.0 / np.pi))
    return 0.5 * x * (1.0 + jnp.tanh(c * (x + 0.044715 * (x * x * x))))


def _rms(x, g):
    return x * lax.rsqrt(jnp.mean(x * x, axis=-1, keepdims=True) + EPS) * g


def _ada_kernel(c_ref, w_ref, b_ref, o_ref):
    c = c_ref[...]
    a = c * jax.nn.sigmoid(c)
    ah, al = _split_bf16(a)
    wh, wl = _split_bf16(w_ref[...])
    dot = functools.partial(jnp.dot, preferred_element_type=F32)
    o_ref[...] = dot(ah, wh) + dot(al, wh) + dot(ah, wl) + b_ref[...]


def _ada(c, w_ada, b_ada):
    rows, d = c.shape
    n = w_ada.shape[1]
    tn = 1024
    return pl.pallas_call(
        _ada_kernel,
        grid=(n // tn,),
        in_specs=[
            pl.BlockSpec((rows, d), lambda j: (0, 0)),
            pl.BlockSpec((d, tn), lambda j: (0, j)),
            pl.BlockSpec((1, tn), lambda j: (0, j)),
        ],
        out_specs=pl.BlockSpec((rows, tn), lambda j: (0, j)),
        out_shape=jax.ShapeDtypeStruct((rows, n), F32),
        compiler_params=_cparams(1),
        name="ada",
    )(c, w_ada, b_ada.reshape(1, n))


def _rope_tables(pos, head_dim):
    rd = head_dim // ROT_FRACTION
    half = rd // 2
    inv = jnp.power(ROPE_THETA, -(jnp.arange(half, dtype=F32) / half))
    ang = pos.astype(F32)[:, None] * inv[None, :]
    cos, sin = jnp.cos(ang), jnp.sin(ang)
    t = pos.shape[0]
    ones = jnp.ones((t, head_dim - rd), F32)
    zeros = jnp.zeros((t, head_dim - rd), F32)
    c = jnp.concatenate([cos, cos, ones], axis=-1)
    s = jnp.concatenate([-sin, sin, zeros], axis=-1)
    reps = LANES // head_dim
    c, s = jnp.tile(c, (1, reps)), jnp.tile(s, (1, reps))
    return jnp.stack([c, s])


def _rope_apply(x, tab_ref, half):
    lane = lax.broadcasted_iota(jnp.int32, x.shape, 1)
    fwd = pltpu.roll(x, LANES - half, 1)
    bwd = pltpu.roll(x, half, 1)
    partner = jnp.where((lane % (2 * half)) < half, fwd, bwd)
    return x * tab_ref[0] + partner * tab_ref[1]


def _proj_kernel(x_ref, sh_ref, sc_ref, gpre_ref, wmain_ref, wih_ref, wil_ref,
                 tq_ref, ti_ref, tk_ref, bgate_ref, gsgu_ref,
                 q_ref, k_ref, v_ref, kbf_ref, vbf_ref, qi_ref, kw_ref, ki4_ref,
                 u_ref, vn_ref, gates_ref):
    nb, t, d = x_ref.shape
    rows = nb * t
    xn = _rms(x_ref[...], gpre_ref[...]) * (1.0 + sc_ref[...]) + sh_ref[...]
    xn = xn.reshape(rows, d)
    xh, xl = _split_bf16(xn)
    dot = functools.partial(jnp.dot, preferred_element_type=F32)
    q_scale = np.float32(HEAD_DIM ** -0.5 * LOG2E)
    qw = N_HEADS * HEAD_DIM
    kvw = N_KV_HEADS * HEAD_DIM
    half_h = HEAD_DIM // ROT_FRACTION // 2
    half_i = IDX_DIM // ROT_FRACTION // 2

    y = dot(xh, wmain_ref[:, 0:qw])
    for h in range(N_HEADS):
        sl = slice(h * HEAD_DIM, (h + 1) * HEAD_DIM)
        q_ref[:, sl] = (_rope_apply(y[:, sl], tq_ref, half_h) * q_scale).astype(BF16)

    y = dot(xh, wmain_ref[:, qw:qw + kvw])
    for h in range(N_KV_HEADS):
        sl = slice(h * HEAD_DIM, (h + 1) * HEAD_DIM)
        kr = _rope_apply(y[:, sl], tq_ref, half_h)
        k_ref[:, sl] = kr
        kbf_ref[:, sl] = kr.astype(BF16)
    y = dot(xh, wmain_ref[:, qw + kvw:qw + 2 * kvw])
    v_ref[...] = y
    vbf_ref[...] = y.astype(BF16)

    wih = wih_ref[...]
    yi = dot(xh, wih) + dot(xl, wih) + dot(xh, wil_ref[...])
    for tile in range(IDX_COLS // LANES):
        sl = slice(tile * LANES, (tile + 1) * LANES)
        qi_ref[:, sl] = _rope_apply(yi[:, sl], ti_ref, half_i)
    kw = _rope_apply(yi[:, IDX_COLS:IDX_PAD], tk_ref, half_i)
    kw_ref[...] = kw
    lane = lax.broadcasted_iota(jnp.int32, kw.shape, 1)
    kk = jnp.where(lane < IDX_DIM, kw, pltpu.roll(kw, IDX_DIM, 1))
    kh, kl = _split_bf16(kk)
    ki4_ref[:, 0:LANES] = kh
    ki4_ref[:, LANES:2 * LANES] = kl

    base = qw + 2 * kvw
    u_ref[...] = _gelu(dot(xh, wmain_ref[:, base:base + d])).astype(u_ref.dtype)
    g = _gelu(dot(xh, wmain_ref[:, base + d:base + 2 * d]))
    mu = jnp.mean(g, axis=-1, keepdims=True)
    gc = g - mu
    vn = gc * lax.rsqrt(jnp.mean(gc * gc, axis=-1, keepdims=True) + EPS) * gsgu_ref[...]
    vn_ref[...] = vn.astype(vn_ref.dtype)

    y = dot(xh, wmain_ref[:, base + 2 * d:base + 4 * d]) + bgate_ref[...]
    gates_ref[...] = jax.nn.sigmoid(y).astype(BF16)


def _proj(x, sh, sc, gpre, wmain, wih, wil, tabs, tab_map, bgate, gsgu, nb, vn_dtype):
    s_total, t_full, d = x.shape
    tq, ti, tk = tabs
    if nb == 1:
        t = min(t_full, 512)
        grid = (s_total, t_full // t)
        xmap = lambda b, j: (b, j, 0)
        mmap = lambda b, j: (b, 0, 0)
        rmap = lambda b, j: (b * (t_full // t) + j, 0)
        tmap = lambda b, j: (0,) + tab_map(j)
    else:
        t = t_full
        grid = (s_total // nb,)
        xmap = lambda b: (b, 0, 0)
        mmap = lambda b: (b, 0, 0)
        rmap = lambda b: (b, 0)
        tmap = lambda b: (0,) + tab_map(0)
    rows = nb * t
    n_rows = s_total * t_full
    qw, kvw = N_HEADS * HEAD_DIM, N_KV_HEADS * HEAD_DIM

    def row_spec(width):
        return pl.BlockSpec((rows, width), rmap)

    def row_shape(width, dtype):
        return jax.ShapeDtypeStruct((n_rows, width), dtype)

    tab_spec = pl.BlockSpec((2, rows, LANES), tmap)
    outs = [(qw, BF16), (kvw, F32), (kvw, F32), (kvw, BF16), (kvw, BF16), (IDX_COLS, F32),
            (LANES, F32), (2 * LANES, BF16), (d, BF16), (d, vn_dtype), (2 * d, BF16)]
    return pl.pallas_call(
        _proj_kernel,
        grid=grid,
        in_specs=[
            pl.BlockSpec((nb, t, d), xmap),
            pl.BlockSpec((nb, 1, d), mmap),
            pl.BlockSpec((nb, 1, d), mmap),
            _const_spec((1, 1, d)),
            _const_spec(wmain.shape),
            _const_spec(wih.shape),
            _const_spec(wil.shape),
            tab_spec, tab_spec, tab_spec,
            _const_spec((1, 2 * d)),
            _const_spec((1, d)),
        ],
        out_specs=[row_spec(w) for w, _ in outs],
        out_shape=[row_shape(w, dt) for w, dt in outs],
        compiler_params=_cparams(len(grid)),
        name="proj",
    )(x, sh, sc, gpre.reshape(1, 1, d), wmain, wih, wil, tq, ti, tk,
      bgate.reshape(1, 2 * d), gsgu.reshape(1, d))


def _dsa_kernel(q_ref, qi_ref, kw_ref, ki4_ref, k_ref, v_ref, o_ref,
                scr, qi4_scr, wb_scr, qg_scr, acc_scr, m_scr, l_scr, thr_scr, lo_scr, hi_scr,
                hit_scr, bias_scr, lg_scr, p_scr, corr_scr, s_scr,
                *, tq, kb, topk, causal, n_keys):
    qb = pl.program_id(1)
    row = lax.broadcasted_iota(jnp.int32, (tq, 1), 0)
    if causal:
        n_adm = ((qb * tq + row) // CHUNK + 1) * CHUNK
        nkb = ((qb + 1) * tq + kb - 1) // kb
    else:
        n_adm = jnp.full((tq, 1), n_keys, jnp.int32)
        nkb = (n_keys + kb - 1) // kb
    lane128 = lax.broadcasted_iota(jnp.int32, (tq, LANES), 1)
    nt = (((1,), (1,)), ((), ()))

    qi = qi_ref[0]
    for h in range(N_IDX_HEADS):
        tile = qi[:, (h // 2) * LANES:(h // 2 + 1) * LANES]
        own = (lane128 < IDX_DIM) if h % 2 == 0 else (lane128 >= IDX_DIM)
        both = jnp.where(own, tile, pltpu.roll(tile, IDX_DIM, 1))
        hi = both.astype(BF16).astype(F32)
        mix = jnp.where(lane128 < IDX_DIM, hi, both - hi).astype(BF16)
        qi4_scr[h * tq:(h + 1) * tq, 0:LANES] = mix
        qi4_scr[h * tq:(h + 1) * tq, LANES:2 * LANES] = mix
        wb_scr[h] = jnp.broadcast_to(kw_ref[0, :, WI_LANE + h:WI_LANE + h + 1], (tq, LANES))
    for g in range(N_KV_HEADS):
        for r in range(HEADS_PER_KV):
            hh = g * HEADS_PER_KV + r
            qg_scr[g, r * tq:(r + 1) * tq, :] = q_ref[0, :, hh * HEAD_DIM:(hh + 1) * HEAD_DIM]

    kh = kb // 2 if (causal and kb % (2 * LANES) == 0) else kb
    n_half = ki4_ref.shape[1] // kh

    def head_products(hidx, slot):
        kblk = ki4_ref[0, pl.ds(pl.multiple_of(hidx * kh, kh), kh), :]
        s_scr[slot] = lax.dot_general(qi4_scr[...], kblk, nt, preferred_element_type=F32)

    def reduce_heads(j, slot, col0, carry):
        rmin, rmax = carry
        for c in range(kh // LANES):
            cs = slice(c * LANES, (c + 1) * LANES)
            acc = jnp.zeros((tq, LANES), F32)
            for h in range(N_IDX_HEADS):
                acc = acc + wb_scr[h] * jnp.maximum(s_scr[slot, h * tq:(h + 1) * tq, cs], 0.0)
            allowed = (j * kb + col0 + c * LANES + lane128) < n_adm
            scr[j, :, col0 + c * LANES:col0 + (c + 1) * LANES] = jnp.where(allowed, acc, NEG)
            rmin = jnp.minimum(rmin, jnp.where(allowed, acc, BIG))
            rmax = jnp.maximum(rmax, jnp.where(allowed, acc, -BIG))
        return rmin, rmax

    carry0 = (jnp.full((tq, LANES), BIG, F32), jnp.full((tq, LANES), -BIG, F32))
    head_products(0, 0)
    if kh < kb:
        def score_step(j, carry):
            head_products(2 * j + 1, 1)
            carry = reduce_heads(j, 0, 0, carry)
            head_products(jnp.minimum(2 * j + 2, n_half - 1), 0)
            return reduce_heads(j, 1, kh, carry)

        rmin, rmax = lax.fori_loop(0, nkb, score_step, carry0)
    else:
        assert nkb == 1
        rmin, rmax = reduce_heads(0, 0, 0, carry0)
    rmin = jnp.min(rmin, axis=1, keepdims=True)
    rmax = jnp.max(rmax, axis=1, keepdims=True)

    def rep(col):
        return jnp.broadcast_to(col, (tq, LANES))

    def over_tiles(tile_fn, init, combine):
        def body(j, acc):
            for c in range(kb // LANES):
                kpos = (j * kb + c * LANES + lane128).astype(F32)
                acc = combine(acc, tile_fn(scr[j, :, c * LANES:(c + 1) * LANES], kpos))
            return acc

        return lax.fori_loop(0, nkb, body, jnp.full((tq, LANES), init, F32))

    def count(pred_fn):
        acc = over_tiles(lambda s, kpos: pred_fn(s, kpos), 0.0, jnp.add)
        return rep(jnp.sum(acc, axis=1, keepdims=True))

    def count_ge(t):
        return count(lambda s, kpos: jnp.where(s >= t, 1.0, 0.0))

    kf = np.float32(topk)
    rmin, rmax = rep(rmin), rep(rmax)
    need = rep(jnp.where(n_adm > topk, 1.0, 0.0))
    lo_scr[...] = rmin
    hi_scr[...] = jnp.where(need > 0.5, rmax, rmin)
    hit_scr[...] = jnp.zeros((tq, LANES), F32)

    def bis_cond(carry):
        it, n_moved = carry
        return jnp.logical_and(it < BISECT_CAP, n_moved > 0.0)

    def bis_body(carry):
        it, _ = carry
        lo, hi = lo_scr[...], hi_scr[...]
        mid = 0.5 * lo + 0.5 * hi
        cnt = count_ge(mid)
        new_lo = jnp.where(cnt >= kf, mid, lo)
        new_hi = jnp.where(cnt > kf, hi, mid)
        lo_scr[...] = new_lo
        hi_scr[...] = new_hi
        hit_scr[...] = jnp.where(cnt == kf, 1.0, hit_scr[...])
        moved = jnp.where(new_lo != lo, 1.0, jnp.where(new_hi != hi, 1.0, 0.0))
        return it + 1, jnp.sum(moved)

    lax.while_loop(bis_cond, bis_body, (jnp.int32(0), jnp.float32(1.0)))
    thr_scr[...] = lo_scr[...]
    tie_f = jnp.where(hit_scr[...] > 0.5, 0.0, need)

    @pl.when(jnp.sum(tie_f) > 0.0)
    def _ties():
        tie = tie_f > 0.5

        def max_below(bound, strict):
            def tile(s, kpos):
                ok = (s < bound) if strict else (s <= bound)
                return jnp.where(ok, s, -BIG)

            return rep(jnp.max(over_tiles(tile, -BIG, jnp.maximum), axis=1, keepdims=True))

        def short(c):
            return jnp.logical_and(tie, c < kf)

        def snap_cond(carry):
            return jnp.sum(jnp.where(short(carry[1]), 1.0, 0.0)) > 0.0

        def snap_body(carry):
            t_old, c_old = carry
            t_new = max_below(t_old, True)
            c_new = count_ge(t_new)
            upd = short(c_old)
            return jnp.where(upd, t_new, t_old), jnp.where(upd, c_new, c_old)

        t0 = max_below(hi_scr[...], False)
        t, _ = lax.while_loop(snap_cond, snap_body, (t0, count_ge(t0)))
        keep = kf - count(lambda s, kpos: jnp.where(s > t, 1.0, 0.0))

        def count_eq_before(cut):
            return count(lambda s, kpos: jnp.where(s == t, jnp.where(kpos < cut, 1.0, 0.0), 0.0))

        def cut_body(_, carry):
            clo, chi = carry
            cmid = jnp.floor(0.5 * (clo + chi))
            ok = count_eq_before(cmid) >= keep
            return jnp.where(ok, clo, cmid), jnp.where(ok, cmid, chi)

        n_total = np.float32(scr.shape[0] * kb)
        steps = int(np.ceil(np.log2(scr.shape[0] * kb))) + 1
        _, cut = lax.fori_loop(0, steps, cut_body, (jnp.zeros((tq, LANES), F32),
                                                    jnp.full((tq, LANES), n_total, F32)))
        cut = jnp.where(tie, cut, n_total)

        def drop(j, carry):
            for c in range(kb // LANES):
                cs = slice(c * LANES, (c + 1) * LANES)
                s = scr[j, :, cs]
                kpos = (j * kb + c * LANES + lane128).astype(F32)
                dropped = jnp.where(s == t, jnp.where(kpos >= cut, 1.0, 0.0), 0.0)
                scr[j, :, cs] = jnp.where(dropped > 0.5, NEG, s)
            return carry

        lax.fori_loop(0, nkb, drop, 0)
        thr_scr[...] = jnp.where(tie, t, thr_scr[...])

    m_scr[...] = jnp.full(m_scr.shape, NEG, F32)
    l_scr[...] = jnp.zeros(l_scr.shape, F32)
    acc_scr[...] = jnp.zeros(acc_scr.shape, F32)
    thr = thr_scr[...]

    rows_g = HEADS_PER_KV * tq
    rt = min(tq, ATT_ROW_TILE)
    cols = [slice(c * LANES, (c + 1) * LANES) for c in range(kb // LANES)]

    def qk(j, g):
        koff = pl.ds(pl.multiple_of(j * kb, kb), kb)
        kg = k_ref[0, koff, g * HEAD_DIM:(g + 1) * HEAD_DIM]
        lg_scr[g] = lax.dot_general(qg_scr[g], kg, nt, preferred_element_type=F32)

    def pv(j, g):
        koff = pl.ds(pl.multiple_of(j * kb, kb), kb)
        vg = v_ref[0, koff, g * HEAD_DIM:(g + 1) * HEAD_DIM]
        acc_scr[g] = acc_scr[g] * corr_scr[g] + jnp.dot(p_scr[g], vg, preferred_element_type=F32)

    def softmax(g):
        for i in range(rows_g // rt):
            rs = slice(i * rt, (i + 1) * rt)
            qs = slice((i * rt) % tq, (i * rt) % tq + rt)
            lg = [lg_scr[g, rs, cs] + bias_scr[qs, cs] for cs in cols]
            blk_max = jnp.max(functools.reduce(jnp.maximum, lg), axis=1, keepdims=True)
            m_old = m_scr[g, rs, :]
            m_new = jnp.maximum(m_old, jnp.broadcast_to(blk_max, (rt, LANES)))
            corr = jnp.exp2(m_old - m_new)
            p_sum = jnp.zeros((rt, LANES), F32)
            for cs, lg_c in zip(cols, lg):
                p = jnp.exp2(lg_c - m_new)
                p_scr[g, rs, cs] = p.astype(BF16)
                p_sum = p_sum + p
            l_scr[g, rs, :] = corr * l_scr[g, rs, :] + p_sum
            corr_scr[g, rs, :] = corr
            m_scr[g, rs, :] = m_new

    def select_bias(j):
        for cs in cols:
            bias_scr[:, cs] = jnp.where(scr[j, :, cs] >= thr, 0.0, NEG)

    if causal:
        def attend(j, carry):
            pv(jnp.maximum(j - 1, 0), 1)
            qk(j, 1)
            select_bias(j)
            softmax(0)
            pv(j, 0)
            qk(jnp.minimum(j + 1, nkb - 1), 0)
            softmax(1)
            return carry

        p_scr[1] = jnp.zeros(p_scr.shape[1:], BF16)
        corr_scr[1] = jnp.zeros(corr_scr.shape[1:], F32)
        qk(0, 0)
        lax.fori_loop(0, nkb, attend, 0)
        pv(nkb - 1, 1)
    else:
        assert nkb == 1
        qk(0, 0)
        qk(0, 1)
        select_bias(0)
        softmax(0)
        pv(0, 0)
        softmax(1)
        pv(0, 1)
    for g in range(N_KV_HEADS):
        o = acc_scr[g] / jnp.sum(l_scr[g], axis=1, keepdims=True)
        for r in range(HEADS_PER_KV):
            hh = g * HEADS_PER_KV + r
            o_ref[0, :, hh * HEAD_DIM:(hh + 1) * HEAD_DIM] = o[r * tq:(r + 1) * tq].astype(o_ref.dtype)


def _dsa(q, qi, kw, ki4, k_bf, v_bf, *, tq, kb, causal, n_keys):
    s_total, t_q, qw = q.shape
    lp = ki4.shape[1]
    assert lp % kb == 0 and t_q % tq == 0 and kb % LANES == 0
    assert (kb % (2 * LANES) == 0) if causal else (lp == kb)
    kh = kb // 2 if causal else kb
    topk = min(TOPK_MAX, n_keys // 4)
    kernel = functools.partial(_dsa_kernel, tq=tq, kb=kb, topk=topk, causal=causal, n_keys=n_keys)
    qmap = lambda b, j: (b, j, 0)
    kmap = lambda b, j: (b, 0, 0)
    rows_g = HEADS_PER_KV * tq
    return pl.pallas_call(
        kernel,
        grid=(s_total, t_q // tq),
        in_specs=[
            pl.BlockSpec((1, tq, qw), qmap),
            pl.BlockSpec((1, tq, IDX_COLS), qmap),
            pl.BlockSpec((1, tq, LANES), qmap),
            pl.BlockSpec((1, lp, 2 * LANES), kmap, pipeline_mode=pl.Buffered(1)),
            pl.BlockSpec((1, lp, k_bf.shape[2]), kmap, pipeline_mode=pl.Buffered(1)),
            pl.BlockSpec((1, lp, v_bf.shape[2]), kmap, pipeline_mode=pl.Buffered(1)),
        ],
        out_specs=pl.BlockSpec((1, tq, qw), qmap),
        out_shape=jax.ShapeDtypeStruct((s_total, t_q, qw), BF16),
        scratch_shapes=[
            pltpu.VMEM((lp // kb, tq, kb), F32),
            pltpu.VMEM((N_IDX_HEADS * tq, 2 * LANES), BF16),
            pltpu.VMEM((N_IDX_HEADS, tq, LANES), F32),
            pltpu.VMEM((N_KV_HEADS, rows_g, HEAD_DIM), BF16),
            pltpu.VMEM((N_KV_HEADS, rows_g, HEAD_DIM), F32),
            pltpu.VMEM((N_KV_HEADS, rows_g, LANES), F32),
            pltpu.VMEM((N_KV_HEADS, rows_g, LANES), F32),
            pltpu.VMEM((tq, LANES), F32),
            pltpu.VMEM((tq, LANES), F32),
            pltpu.VMEM((tq, LANES), F32),
            pltpu.VMEM((tq, LANES), F32),
            pltpu.VMEM((tq, kb), F32),
            pltpu.VMEM((N_KV_HEADS, rows_g, kb), F32),
            pltpu.VMEM((N_KV_HEADS, rows_g, kb), BF16),
            pltpu.VMEM((N_KV_HEADS, rows_g, LANES), F32),
            pltpu.VMEM((2, N_IDX_HEADS * tq, kh), F32),
        ],
        compiler_params=_cparams(2),
        name="dsa",
    )(q, qi, kw, ki4, k_bf, v_bf)


def _mix_kernel(x_ref, gt_ref, gpost_ref, att_ref, u_ref, vn_ref, gates_ref, ws_ref, bs_ref,
                wpa_ref, wpb_ref, wout_ref, o_ref, sgu_scr):
    nb, t, d = x_ref.shape
    rows = nb * t
    gd = d // N_SGU_GROUPS
    dot = functools.partial(jnp.dot, preferred_element_type=F32)
    r = lax.broadcasted_iota(jnp.int32, (MLP_CHUNK, MLP_CHUNK), 0)
    c = lax.broadcasted_iota(jnp.int32, (MLP_CHUNK, MLP_CHUNK), 1)
    for g in range(N_SGU_GROUPS):
        wg = jnp.where(r >= c, ws_ref[g], 0.0).astype(BF16)
        gs = slice(g * gd, (g + 1) * gd)
        for ch in range(rows // MLP_CHUNK):
            rs = slice(ch * MLP_CHUNK, (ch + 1) * MLP_CHUNK)
            mixed = dot(wg, vn_ref[rs, gs].astype(BF16)) + bs_ref[:, gs]
            sgu_scr[rs, gs] = (u_ref[rs, gs].astype(F32) * mixed).astype(BF16)
    a = dot(att_ref[...], wpa_ref[...])
    b = dot(sgu_scr[...], wpb_ref[...])
    merged = gates_ref[:, 0:d].astype(F32) * a + gates_ref[:, d:2 * d].astype(F32) * b
    h = dot(merged.astype(BF16), wout_ref[...])
    hn = _rms(h, gpost_ref[...])
    o_ref[...] = x_ref[...] + gt_ref[...] * hn.reshape(nb, t, d)


def _row_blocking(s_total, t_full, nb):
    if nb == 1:
        t = min(t_full, 512)
        grid = (s_total, t_full // t)
        xmap = lambda b, j: (b, j, 0)
        mmap = lambda b, j: (b, 0, 0)
        rmap = lambda b, j: (b * (t_full // t) + j, 0)
    else:
        t = t_full
        grid = (s_total // nb,)
        xmap = lambda b: (b, 0, 0)
        mmap = lambda b: (b, 0, 0)
        rmap = lambda b: (b, 0)
    return t, grid, xmap, mmap, rmap


def _mix(x, gt, gpost, att, u, vn, gates, ws, bs, wpa, wpb, wout, nb):
    s_total, t_full, d = x.shape
    t, grid, xmap, mmap, rmap = _row_blocking(s_total, t_full, nb)
    rows = nb * t
    return pl.pallas_call(
        _mix_kernel,
        grid=grid,
        in_specs=[
            pl.BlockSpec((nb, t, d), xmap),
            pl.BlockSpec((nb, 1, d), mmap),
            _const_spec((1, 1, d)),
            pl.BlockSpec((rows, d), rmap),
            pl.BlockSpec((rows, d), rmap),
            pl.BlockSpec((rows, d), rmap),
            pl.BlockSpec((rows, 2 * d), rmap),
            _const_spec(ws.shape),
            _const_spec(bs.shape),
            _const_spec(wpa.shape),
            _const_spec(wpb.shape),
            _const_spec(wout.shape),
        ],
        out_specs=pl.BlockSpec((nb, t, d), xmap),
        out_shape=jax.ShapeDtypeStruct(x.shape, F32),
        scratch_shapes=[pltpu.VMEM((rows, d), BF16)],
        compiler_params=_cparams(len(grid)),
        name="mix",
    )(x, gt, gpost.reshape(1, 1, d), att, u, vn, gates, ws, bs, wpa, wpb, wout)


def _ffn_kernel(x_ref, sh_ref, sc_ref, gt_ref, gpre_ref, gpost_ref, win_ref, wout_ref, o_ref,
                *, n_split):
    nb, t, d = x_ref.shape
    rows = nb * t
    hidden = wout_ref.shape[0]
    hc = hidden // n_split
    dot = functools.partial(jnp.dot, preferred_element_type=F32)
    x = x_ref[...]
    xn = (_rms(x, gpre_ref[...]) * (1.0 + sc_ref[...]) + sh_ref[...]).reshape(rows, d).astype(BF16)
    acc = jnp.zeros((rows, d), F32)
    for i in range(n_split):
        gg = dot(xn, win_ref[:, i * hc:(i + 1) * hc])
        up = dot(xn, win_ref[:, hidden + i * hc:hidden + (i + 1) * hc])
        hmid = (gg * jax.nn.sigmoid(gg) * up).astype(BF16)
        acc = acc + dot(hmid, wout_ref[i * hc:(i + 1) * hc, :])
    hn = _rms(acc, gpost_ref[...])
    o_ref[...] = x + gt_ref[...] * hn.reshape(nb, t, d)


def _ffn(x, sh, sc, gt, gpre, gpost, win, wout, nb):
    s_total, t_full, d = x.shape
    t, grid, xmap, mmap, _ = _row_blocking(s_total, t_full, nb)
    hidden = wout.shape[0]
    n_split = 2
    assert hidden % (n_split * LANES) == 0
    return pl.pallas_call(
        functools.partial(_ffn_kernel, n_split=n_split),
        grid=grid,
        in_specs=[
            pl.BlockSpec((nb, t, d), xmap),
            pl.BlockSpec((nb, 1, d), mmap),
            pl.BlockSpec((nb, 1, d), mmap),
            pl.BlockSpec((nb, 1, d), mmap),
            _const_spec((1, 1, d)),
            _const_spec((1, 1, d)),
            _const_spec(win.shape),
            _const_spec(wout.shape),
        ],
        out_specs=pl.BlockSpec((nb, t, d), xmap),
        out_shape=jax.ShapeDtypeStruct(x.shape, F32),
        compiler_params=_cparams(len(grid)),
        name="ffn",
    )(x, sh, sc, gt, gpre.reshape(1, 1, d), gpost.reshape(1, 1, d), win, wout)


def _prep_weights(w_in, w_proj_a, w_proj_b, w_out, w_ffn_in, w_ffn_out, d):
    qw, kvw = N_HEADS * HEAD_DIM, N_KV_HEADS * HEAD_DIM
    o_q, o_k, o_v = 0, qw, qw + kvw
    o_qi = qw + 2 * kvw
    o_ki = o_qi + IDX_COLS
    o_wi = o_ki + IDX_DIM
    o_u = o_wi + N_IDX_HEADS
    wmain = jnp.concatenate([w_in[:, o_q:o_qi], w_in[:, o_u:]], axis=1).astype(BF16)
    pad = jnp.zeros((d, IDX_PAD - IDX_COLS - IDX_DIM - N_IDX_HEADS), F32)
    widx = jnp.concatenate([w_in[:, o_qi:o_u], pad], axis=1)
    wih, wil = _split_bf16(widx)
    return dict(
        wmain=wmain, wih=wih, wil=wil,
        wpa=w_proj_a.astype(BF16), wpb=w_proj_b.astype(BF16), wout=w_out.astype(BF16),
        wfi=w_ffn_in.astype(BF16), wfo=w_ffn_out.astype(BF16),
    )


def _spatial_tables(w_spatial, b_spatial, seq_len, d):
    cl = min(seq_len, MLP_CHUNK)
    reps = MLP_CHUNK // cl
    gd = d // N_SGU_GROUPS
    w = w_spatial[:, :cl, :cl]
    if reps > 1:
        eye = jnp.eye(reps, dtype=w.dtype)
        w = jnp.einsum("ab,gts->gatbs", eye, w).reshape(N_SGU_GROUPS, MLP_CHUNK, MLP_CHUNK)
    b = jnp.tile(b_spatial[:, :cl].T, (reps, 1))
    b = jnp.repeat(b, gd, axis=1)
    return w, b


def _stream(x, mod, pos, wts, g_pre_mix, g_post_mix, g_pre_ffn, g_post_ffn, b_gate, g_sgu,
            w_spatial, b_spatial, past, nb):
    s_total, t_full, d = x.shape
    sh1, sc1, gt1, sh2, sc2, gt2 = [m[:, None, :] for m in jnp.split(mod, 6, axis=-1)]

    idx_scale = np.float32((IDX_DIM * N_IDX_HEADS) ** -0.5)
    kw_scale = jnp.ones((LANES,), F32).at[WI_LANE:WI_LANE + N_IDX_HEADS].set(idx_scale)
    if nb == 1:
        tpos = pos
        tab_map = lambda j: (j, 0)
    else:
        tpos = jnp.tile(pos, nb)
        tab_map = lambda j: (0, 0)
    tabs = (_rope_tables(tpos, HEAD_DIM), _rope_tables(tpos, IDX_DIM), _kw_tables(tpos, kw_scale))

    vn_dtype = F32 if past is not None else BF16
    (q, k, v, k_bf, v_bf, qi, kw, ki4, u, vn, gates) = _proj(
        x, sh1, sc1, g_pre_mix, wts["wmain"], wts["wih"], wts["wil"], tabs, tab_map,
        b_gate, g_sgu, nb, vn_dtype)

    shp = lambda a: a.reshape(s_total, t_full, a.shape[-1])
    if past is None:
        n_keys = t_full
        att = _dsa(shp(q), shp(qi), shp(kw), shp(ki4), shp(k_bf), shp(v_bf),
                   tq=min(DSA_Q_BLOCK, t_full), kb=min(DSA_KEY_BLOCK, t_full), causal=True,
                   n_keys=n_keys)
    else:
        pk, pv, pki = past
        p_len = pk.shape[1]
        n_keys = p_len + t_full
        kb = LANES * ((n_keys + LANES - 1) // LANES)
        padn = kb - n_keys
        pkh, pkl = _split_bf16(pki)
        ki4_all = jnp.concatenate(
            [jnp.concatenate([pkh, pkh, pkl, pkl], axis=-1), shp(ki4)], axis=1)
        cat = lambda a, b_: jnp.pad(jnp.concatenate([a.astype(BF16), b_], axis=1),
                                    ((0, 0), (0, padn), (0, 0)))
        ki4_all = jnp.pad(ki4_all, ((0, 0), (0, padn), (0, 0)))
        att = _dsa(shp(q), shp(qi), shp(kw), ki4_all,
                   cat(pk.reshape(s_total, p_len, -1), shp(k_bf)),
                   cat(pv.reshape(s_total, p_len, -1), shp(v_bf)),
                   tq=t_full, kb=kb, causal=False, n_keys=n_keys)
    att = att.reshape(s_total * t_full, -1)

    ws, bs = _spatial_tables(w_spatial, b_spatial, t_full, d)
    x1 = _mix(x, gt1, g_post_mix, att, u, vn, gates, ws, bs,
              wts["wpa"], wts["wpb"], wts["wout"], nb)
    x2 = _ffn(x1, sh2, sc2, gt2, g_pre_ffn, g_post_ffn, wts["wfi"], wts["wfo"], nb)
    return x2, shp(k), shp(v), shp(kw)[..., :IDX_DIM], shp(vn)


def _kw_tables(pos, scale_lanes):
    tab = _rope_tables(pos, IDX_DIM)
    lane = jnp.arange(LANES)
    c = jnp.where(lane[None, :] < IDX_DIM, tab[0], 1.0) * scale_lanes[None, :]
    s = jnp.where(lane[None, :] < IDX_DIM, tab[1], 0.0)
    return jnp.stack([c, s])


def kernel(x_prompt, x_sample, cache_k, cache_v, cache_idx_k, c_prompt, c_sample, w_ada, b_ada,
           g_pre_mix, g_post_mix, g_pre_ffn, g_post_ffn, w_in, b_gate, g_sgu, w_spatial,
           b_spatial, w_proj_a, w_proj_b, w_out, w_ffn_in, w_ffn_out):
    depth = w_ada.shape[0]
    bp, t_p, d = x_prompt.shape
    bs_, t_s, _ = x_sample.shape
    past_len = cache_k.shape[2]
    pos_p = jnp.arange(t_p, dtype=jnp.int32)
    pos_s = past_len + jnp.arange(t_s, dtype=jnp.int32)
    nb_s = max(1, min(bs_, 256 // t_s))

    xp, xs = x_prompt, x_sample
    outs = [[] for _ in range(7)]
    for l in range(depth):
        wts = _prep_weights(w_in[l], w_proj_a[l], w_proj_b[l], w_out[l], w_ffn_in[l],
                            w_ffn_out[l], d)
        c_all = jnp.concatenate([c_prompt, c_sample], axis=0)
        pad_rows = (-c_all.shape[0]) % 8
        c_all = jnp.pad(c_all, ((0, pad_rows), (0, 0)))
        mod = _ada(c_all, w_ada[l], b_ada[l])
        common = (wts, g_pre_mix[l], g_post_mix[l], g_pre_ffn[l], g_post_ffn[l], b_gate[l],
                  g_sgu[l], w_spatial[l], b_spatial[l])
        xp, kp, vp, kip, _ = _stream(xp, mod[:bp], pos_p, *common, None, 1)
        past = (cache_k[l], cache_v[l], cache_idx_k[l])
        xs, ks, vs, kis, vns = _stream(xs, mod[bp:bp + bs_], pos_s, *common, past, nb_s)
        hd = (N_KV_HEADS, HEAD_DIM)
        for lst, val in zip(outs, (kp.reshape(bp, t_p, *hd), vp.reshape(bp, t_p, *hd), kip,
                                   ks.reshape(bs_, t_s, *hd), vs.reshape(bs_, t_s, *hd), kis,
                                   vns)):
            lst.append(val)
    return (xp, xs) + tuple(jnp.stack(o) for o in outs)
```

```python
import functools

import jax
import jax.numpy as jnp
import numpy as np
from jax import lax
from jax.experimental import pallas as pl
from jax.experimental.pallas import tpu as pltpu

F32 = jnp.float32
BF16 = jnp.bfloat16

CHUNK = 64
N_HEADS = 8
N_KV_HEADS = 2
HEAD_DIM = 128
N_IDX_HEADS = 8
IDX_DIM = 64
TOPK_MAX = 256
ROPE_THETA = 500000.0
ROT_FRACTION = 4
MLP_CHUNK = 128
N_SGU_GROUPS = 8
NEG = -1e30
BIG = 3e38
EPS = 1e-6

LANES = 128
VMEM_LIMIT_BYTES = 56 * 1024 * 1024

HEADS_PER_KV = N_HEADS // N_KV_HEADS
IDX_COLS = N_IDX_HEADS * IDX_DIM
IDX_PAD = IDX_COLS + LANES
WI_LANE = IDX_DIM
DSA_Q_BLOCK = 128
DSA_KEY_BLOCK = 512
ATT_ROW_TILE = 32
LOG2E = float(np.log2(np.e))
BISECT_CAP = 40


def _cparams(n_axes):
    return pltpu.CompilerParams(
        dimension_semantics=("arbitrary",) * n_axes,
        vmem_limit_bytes=VMEM_LIMIT_BYTES,
    )


def _const_spec(shape):
    nd = len(shape)
    return pl.BlockSpec(shape, lambda *_: (0,) * nd, pipeline_mode=pl.Buffered(1))


def _split_bf16(x):
    hi = x.astype(BF16)
    lo = (x - hi.astype(F32)).astype(BF16)
    return hi, lo


def _gelu(x):
    c = np.float32(np.sqrt(2.0 / np.pi))
    return 0.5 * x * (1.0 + jnp.tanh(c * (x + 0.044715 * (x * x * x))))


def _rms(x, g):
    return x * lax.rsqrt(jnp.mean(x * x, axis=-1, keepdims=True) + EPS) * g


def _ada_kernel(c_ref, w_ref, b_ref, o_ref):
    c = c_ref[...]
    a = c * jax.nn.sigmoid(c)
    ah, al = _split_bf16(a)
    wh, wl = _split_bf16(w_ref[...])
    dot = functools.partial(jnp.dot, preferred_element_type=F32)
    o_ref[...] = dot(ah, wh) + dot(al, wh) + dot(ah, wl) + b_ref[...]


def _ada(c, w_ada, b_ada):
    rows, d = c.shape
    n = w_ada.shape[1]
    tn = 1024
    return pl.pallas_call(
        _ada_kernel,
        grid=(n // tn,),
        in_specs=[
            pl.BlockSpec((rows, d), lambda j: (0, 0)),
            pl.BlockSpec((d, tn), lambda j: (0, j)),
            pl.BlockSpec((1, tn), lambda j: (0, j)),
        ],
        out_specs=pl.BlockSpec((rows, tn), lambda j: (0, j)),
        out_shape=jax.ShapeDtypeStruct((rows, n), F32),
        compiler_params=_cparams(1),
        name="ada",
    )(c, w_ada, b_ada.reshape(1, n))


def _rope_tables(pos, head_dim):
    rd = head_dim // ROT_FRACTION
    half = rd // 2
    inv = jnp.power(ROPE_THETA, -(jnp.arange(half, dtype=F32) / half))
    ang = pos.astype(F32)[:, None] * inv[None, :]
    cos, sin = jnp.cos(ang), jnp.sin(ang)
    t = pos.shape[0]
    ones = jnp.ones((t, head_dim - rd), F32)
    zeros = jnp.zeros((t, head_dim - rd), F32)
    c = jnp.concatenate([cos, cos, ones], axis=-1)
    s = jnp.concatenate([-sin, sin, zeros], axis=-1)
    reps = LANES // head_dim
    c, s = jnp.tile(c, (1, reps)), jnp.tile(s, (1, reps))
    return jnp.stack([c, s])


def _rope_apply(x, tab_ref, half):
    lane = lax.broadcasted_iota(jnp.int32, x.shape, 1)
    fwd = pltpu.roll(x, LANES - half, 1)
    bwd = pltpu.roll(x, half, 1)
    partner = jnp.where((lane % (2 * half)) < half, fwd, bwd)
    return x * tab_ref[0] + partner * tab_ref[1]


def _proj_kernel(x_ref, sh_ref, sc_ref, gpre_ref, wmain_ref, wih_ref, wil_ref,
                 tq_ref, ti_ref, tk_ref, bgate_ref, gsgu_ref,
                 q_ref, k_ref, v_ref, kbf_ref, vbf_ref, qi_ref, kw_ref, ki4_ref,
                 u_ref, vn_ref, gates_ref):
    nb, t, d = x_ref.shape
    rows = nb * t
    xn = _rms(x_ref[...], gpre_ref[...]) * (1.0 + sc_ref[...]) + sh_ref[...]
    xn = xn.reshape(rows, d)
    xh, xl = _split_bf16(xn)
    dot = functools.partial(jnp.dot, preferred_element_type=F32)
    q_scale = np.float32(HEAD_DIM ** -0.5 * LOG2E)
    qw = N_HEADS * HEAD_DIM
    kvw = N_KV_HEADS * HEAD_DIM
    half_h = HEAD_DIM // ROT_FRACTION // 2
    half_i = IDX_DIM // ROT_FRACTION // 2

    y = dot(xh, wmain_ref[:, 0:qw])
    for h in range(N_HEADS):
        sl = slice(h * HEAD_DIM, (h + 1) * HEAD_DIM)
        q_ref[:, sl] = (_rope_apply(y[:, sl], tq_ref, half_h) * q_scale).astype(BF16)

    y = dot(xh, wmain_ref[:, qw:qw + kvw])
    for h in range(N_KV_HEADS):
        sl = slice(h * HEAD_DIM, (h + 1) * HEAD_DIM)
        kr = _rope_apply(y[:, sl], tq_ref, half_h)
        k_ref[:, sl] = kr
        kbf_ref[:, sl] = kr.astype(BF16)
    y = dot(xh, wmain_ref[:, qw + kvw:qw + 2 * kvw])
    v_ref[...] = y
    vbf_ref[...] = y.astype(BF16)

    wih = wih_ref[...]
    yi = dot(xh, wih) + dot(xl, wih) + dot(xh, wil_ref[...])
    for tile in range(IDX_COLS // LANES):
        sl = slice(tile * LANES, (tile + 1) * LANES)
        qi_ref[:, sl] = _rope_apply(yi[:, sl], ti_ref, half_i)
    kw = _rope_apply(yi[:, IDX_COLS:IDX_PAD], tk_ref, half_i)
    kw_ref[...] = kw
    lane = lax.broadcasted_iota(jnp.int32, kw.shape, 1)
    kk = jnp.where(lane < IDX_DIM, kw, pltpu.roll(kw, IDX_DIM, 1))
    kh, kl = _split_bf16(kk)
    ki4_ref[:, 0:LANES] = kh
    ki4_ref[:, LANES:2 * LANES] = kl

    base = qw + 2 * kvw
    u_ref[...] = _gelu(dot(xh, wmain_ref[:, base:base + d])).astype(u_ref.dtype)
    g = _gelu(dot(xh, wmain_ref[:, base + d:base + 2 * d]))
    mu = jnp.mean(g, axis=-1, keepdims=True)
    gc = g - mu
    vn = gc * lax.rsqrt(jnp.mean(gc * gc, axis=-1, keepdims=True) + EPS) * gsgu_ref[...]
    vn_ref[...] = vn.astype(vn_ref.dtype)

    y = dot(xh, wmain_ref[:, base + 2 * d:base + 4 * d]) + bgate_ref[...]
    gates_ref[...] = jax.nn.sigmoid(y).astype(BF16)


def _proj(x, sh, sc, gpre, wmain, wih, wil, tabs, tab_map, bgate, gsgu, nb, vn_dtype):
    s_total, t_full, d = x.shape
    tq, ti, tk = tabs
    if nb == 1:
        t = min(t_full, 512)
        grid = (s_total, t_full // t)
        xmap = lambda b, j: (b, j, 0)
        mmap = lambda b, j: (b, 0, 0)
        rmap = lambda b, j: (b * (t_full // t) + j, 0)
        tmap = lambda b, j: (0,) + tab_map(j)
    else:
        t = t_full
        grid = (s_total // nb,)
        xmap = lambda b: (b, 0, 0)
        mmap = lambda b: (b, 0, 0)
        rmap = lambda b: (b, 0)
        tmap = lambda b: (0,) + tab_map(0)
    rows = nb * t
    n_rows = s_total * t_full
    qw, kvw = N_HEADS * HEAD_DIM, N_KV_HEADS * HEAD_DIM

    def row_spec(width):
        return pl.BlockSpec((rows, width), rmap)

    def row_shape(width, dtype):
        return jax.ShapeDtypeStruct((n_rows, width), dtype)

    tab_spec = pl.BlockSpec((2, rows, LANES), tmap)
    outs = [(qw, BF16), (kvw, F32), (kvw, F32), (kvw, BF16), (kvw, BF16), (IDX_COLS, F32),
            (LANES, F32), (2 * LANES, BF16), (d, BF16), (d, vn_dtype), (2 * d, BF16)]
    return pl.pallas_call(
        _proj_kernel,
        grid=grid,
        in_specs=[
            pl.BlockSpec((nb, t, d), xmap),
            pl.BlockSpec((nb, 1, d), mmap),
            pl.BlockSpec((nb, 1, d), mmap),
            _const_spec((1, 1, d)),
            _const_spec(wmain.shape),
            _const_spec(wih.shape),
            _const_spec(wil.shape),
            tab_spec, tab_spec, tab_spec,
            _const_spec((1, 2 * d)),
            _const_spec((1, d)),
        ],
        out_specs=[row_spec(w) for w, _ in outs],
        out_shape=[row_shape(w, dt) for w, dt in outs],
        compiler_params=_cparams(len(grid)),
        name="proj",
    )(x, sh, sc, gpre.reshape(1, 1, d), wmain, wih, wil, tq, ti, tk,
      bgate.reshape(1, 2 * d), gsgu.reshape(1, d))


def _dsa_kernel(q_ref, qi_ref, kw_ref, ki4_ref, k_ref, v_ref, o_ref,
                scr, qi4_scr, wb_scr, qg_scr, acc_scr, m_scr, l_scr, thr_scr, lo_scr, hi_scr,
                hit_scr, bias_scr, lg_scr, p_scr, corr_scr, s_scr,
                *, tq, kb, topk, causal, n_keys):
    qb = pl.program_id(1)
    row = lax.broadcasted_iota(jnp.int32, (tq, 1), 0)
    if causal:
        n_adm = ((qb * tq + row) // CHUNK + 1) * CHUNK
        nkb = ((qb + 1) * tq + kb - 1) // kb
    else:
        n_adm = jnp.full((tq, 1), n_keys, jnp.int32)
        nkb = (n_keys + kb - 1) // kb
    lane128 = lax.broadcasted_iota(jnp.int32, (tq, LANES), 1)
    nt = (((1,), (1,)), ((), ()))

    qi = qi_ref[0]
    for h in range(N_IDX_HEADS):
        tile = qi[:, (h // 2) * LANES:(h // 2 + 1) * LANES]
        own = (lane128 < IDX_DIM) if h % 2 == 0 else (lane128 >= IDX_DIM)
        both = jnp.where(own, tile, pltpu.roll(tile, IDX_DIM, 1))
        hi = both.astype(BF16).astype(F32)
        mix = jnp.where(lane128 < IDX_DIM, hi, both - hi).astype(BF16)
        qi4_scr[h * tq:(h + 1) * tq, 0:LANES] = mix
        qi4_scr[h * tq:(h + 1) * tq, LANES:2 * LANES] = mix
        wb_scr[h] = jnp.broadcast_to(kw_ref[0, :, WI_LANE + h:WI_LANE + h + 1], (tq, LANES))
    for g in range(N_KV_HEADS):
        for r in range(HEADS_PER_KV):
            hh = g * HEADS_PER_KV + r
            qg_scr[g, r * tq:(r + 1) * tq, :] = q_ref[0, :, hh * HEAD_DIM:(hh + 1) * HEAD_DIM]

    kh = kb // 2 if (causal and kb % (2 * LANES) == 0) else kb
    n_half = ki4_ref.shape[1] // kh

    def head_products(hidx, slot):
        kblk = ki4_ref[0, pl.ds(pl.multiple_of(hidx * kh, kh), kh), :]
        s_scr[slot] = lax.dot_general(qi4_scr[...], kblk, nt, preferred_element_type=F32)

    def reduce_heads(j, slot, col0, carry):
        rmin, rmax = carry
        for c in range(kh // LANES):
            cs = slice(c * LANES, (c + 1) * LANES)
            acc = jnp.zeros((tq, LANES), F32)
            for h in range(N_IDX_HEADS):
                acc = acc + wb_scr[h] * jnp.maximum(s_scr[slot, h * tq:(h + 1) * tq, cs], 0.0)
            allowed = (j * kb + col0 + c * LANES + lane128) < n_adm
            scr[j, :, col0 + c * LANES:col0 + (c + 1) * LANES] = jnp.where(allowed, acc, NEG)
            rmin = jnp.minimum(rmin, jnp.where(allowed, acc, BIG))
            rmax = jnp.maximum(rmax, jnp.where(allowed, acc, -BIG))
        return rmin, rmax

    carry0 = (jnp.full((tq, LANES), BIG, F32), jnp.full((tq, LANES), -BIG, F32))
    head_products(0, 0)
    if kh < kb:
        def score_step(j, carry):
            head_products(2 * j + 1, 1)
            carry = reduce_heads(j, 0, 0, carry)
            head_products(jnp.minimum(2 * j + 2, n_half - 1), 0)
            return reduce_heads(j, 1, kh, carry)

        rmin, rmax = lax.fori_loop(0, nkb, score_step, carry0)
    else:
        assert nkb == 1
        rmin, rmax = reduce_heads(0, 0, 0, carry0)
    rmin = jnp.min(rmin, axis=1, keepdims=True)
    rmax = jnp.max(rmax, axis=1, keepdims=True)

    def rep(col):
        return jnp.broadcast_to(col, (tq, LANES))

    def over_tiles(tile_fn, init, combine):
        def body(j, acc):
            for c in range(kb // LANES):
                kpos = (j * kb + c * LANES + lane128).astype(F32)
                acc = combine(acc, tile_fn(scr[j, :, c * LANES:(c + 1) * LANES], kpos))
            return acc

        return lax.fori_loop(0, nkb, body, jnp.full((tq, LANES), init, F32))

    def count(pred_fn):
        acc = over_tiles(lambda s, kpos: pred_fn(s, kpos), 0.0, jnp.add)
        return rep(jnp.sum(acc, axis=1, keepdims=True))

    def count_ge(t):
        return count(lambda s, kpos: jnp.where(s >= t, 1.0, 0.0))

    kf = np.float32(topk)
    rmin, rmax = rep(rmin), rep(rmax)
    need = rep(jnp.where(n_adm > topk, 1.0, 0.0))
    lo_scr[...] = rmin
    hi_scr[...] = jnp.where(need > 0.5, rmax, rmin)
    hit_scr[...] = jnp.zeros((tq, LANES), F32)

    def bis_cond(carry):
        it, n_moved = carry
        return jnp.logical_and(it < BISECT_CAP, n_moved > 0.0)

    def bis_body(carry):
        it, _ = carry
        lo, hi = lo_scr[...], hi_scr[...]
        mid = 0.5 * lo + 0.5 * hi
        cnt = count_ge(mid)
        new_lo = jnp.where(cnt >= kf, mid, lo)
        new_hi = jnp.where(cnt > kf, hi, mid)
        lo_scr[...] = new_lo
        hi_scr[...] = new_hi
        hit_scr[...] = jnp.where(cnt == kf, 1.0, hit_scr[...])
        moved = jnp.where(new_lo != lo, 1.0, jnp.where(new_hi != hi, 1.0, 0.0))
        return it + 1, jnp.sum(moved)

    lax.while_loop(bis_cond, bis_body, (jnp.int32(0), jnp.float32(1.0)))
    thr_scr[...] = lo_scr[...]
    tie_f = jnp.where(hit_scr[...] > 0.5, 0.0, need)

    @pl.when(jnp.sum(tie_f) > 0.0)
    def _ties():
        tie = tie_f > 0.5

        def max_below(bound, strict):
            def tile(s, kpos):
                ok = (s < bound) if strict else (s <= bound)
                return jnp.where(ok, s, -BIG)

            return rep(jnp.max(over_tiles(tile, -BIG, jnp.maximum), axis=1, keepdims=True))

        def short(c):
            return jnp.logical_and(tie, c < kf)

        def snap_cond(carry):
            return jnp.sum(jnp.where(short(carry[1]), 1.0, 0.0)) > 0.0

        def snap_body(carry):
            t_old, c_old = carry
            t_new = max_below(t_old, True)
            c_new = count_ge(t_new)
            upd = short(c_old)
            return jnp.where(upd, t_new, t_old), jnp.where(upd, c_new, c_old)

        t0 = max_below(hi_scr[...], False)
        t, _ = lax.while_loop(snap_cond, snap_body, (t0, count_ge(t0)))
        keep = kf - count(lambda s, kpos: jnp.where(s > t, 1.0, 0.0))

        def count_eq_before(cut):
            return count(lambda s, kpos: jnp.where(s == t, jnp.where(kpos < cut, 1.0, 0.0), 0.0))

        def cut_body(_, carry):
            clo, chi = carry
            cmid = jnp.floor(0.5 * (clo + chi))
            ok = count_eq_before(cmid) >= keep
            return jnp.where(ok, clo, cmid), jnp.where(ok, cmid, chi)

        n_total = np.float32(scr.shape[0] * kb)
        steps = int(np.ceil(np.log2(scr.shape[0] * kb))) + 1
        _, cut = lax.fori_loop(0, steps, cut_body, (jnp.zeros((tq, LANES), F32),
                                                    jnp.full((tq, LANES), n_total, F32)))
        cut = jnp.where(tie, cut, n_total)

        def drop(j, carry):
            for c in range(kb // LANES):
                cs = slice(c * LANES, (c + 1) * LANES)
                s = scr[j, :, cs]
                kpos = (j * kb + c * LANES + lane128).astype(F32)
                dropped = jnp.where(s == t, jnp.where(kpos >= cut, 1.0, 0.0), 0.0)
                scr[j, :, cs] = jnp.where(dropped > 0.5, NEG, s)
            return carry

        lax.fori_loop(0, nkb, drop, 0)
        thr_scr[...] = jnp.where(tie, t, thr_scr[...])

    m_scr[...] = jnp.full(m_scr.shape, NEG, F32)
    l_scr[...] = jnp.zeros(l_scr.shape, F32)
    acc_scr[...] = jnp.zeros(acc_scr.shape, F32)
    thr = thr_scr[...]

    rows_g = HEADS_PER_KV * tq
    rt = min(tq, ATT_ROW_TILE)
    cols = [slice(c * LANES, (c + 1) * LANES) for c in range(kb // LANES)]

    def qk(j, g):
        koff = pl.ds(pl.multiple_of(j * kb, kb), kb)
        kg = k_ref[0, koff, g * HEAD_DIM:(g + 1) * HEAD_DIM]
        lg_scr[g] = lax.dot_general(qg_scr[g], kg, nt, preferred_element_type=F32)

    def pv(j, g):
        koff = pl.ds(pl.multiple_of(j * kb, kb), kb)
        vg = v_ref[0, koff, g * HEAD_DIM:(g + 1) * HEAD_DIM]
        acc_scr[g] = acc_scr[g] * corr_scr[g] + jnp.dot(p_scr[g], vg, preferred_element_type=F32)

    def softmax(g):
        for i in range(rows_g // rt):
            rs = slice(i * rt, (i + 1) * rt)
            qs = slice((i * rt) % tq, (i * rt) % tq + rt)
            lg = [lg_scr[g, rs, cs] + bias_scr[qs, cs] for cs in cols]
            blk_max = jnp.max(functools.reduce(jnp.maximum, lg), axis=1, keepdims=True)
            m_old = m_scr[g, rs, :]
            m_new = jnp.maximum(m_old, jnp.broadcast_to(blk_max, (rt, LANES)))
            corr = jnp.exp2(m_old - m_new)
            p_sum = jnp.zeros((rt, LANES), F32)
            for cs, lg_c in zip(cols, lg):
                p = jnp.exp2(lg_c - m_new)
                p_scr[g, rs, cs] = p.astype(BF16)
                p_sum = p_sum + p
            l_scr[g, rs, :] = corr * l_scr[g, rs, :] + p_sum
            corr_scr[g, rs, :] = corr
            m_scr[g, rs, :] = m_new

    def select_bias(j):
        for cs in cols:
            bias_scr[:, cs] = jnp.where(scr[j, :, cs] >= thr, 0.0, NEG)

    if causal:
        def attend(j, carry):
            pv(jnp.maximum(j - 1, 0), 1)
            qk(j, 1)
            select_bias(j)
            softmax(0)
            pv(j, 0)
            qk(jnp.minimum(j + 1, nkb - 1), 0)
            softmax(1)
            return carry

        p_scr[1] = jnp.zeros(p_scr.shape[1:], BF16)
        corr_scr[1] = jnp.zeros(corr_scr.shape[1:], F32)
        qk(0, 0)
        lax.fori_loop(0, nkb, attend, 0)
        pv(nkb - 1, 1)
    else:
        assert nkb == 1
        qk(0, 0)
        qk(0, 1)
        select_bias(0)
        softmax(0)
        pv(0, 0)
        softmax(1)
        pv(0, 1)
    for g in range(N_KV_HEADS):
        o = acc_scr[g] / jnp.sum(l_scr[g], axis=1, keepdims=True)
        for r in range(HEADS_PER_KV):
            hh = g * HEADS_PER_KV + r
            o_ref[0, :, hh * HEAD_DIM:(hh + 1) * HEAD_DIM] = o[r * tq:(r + 1) * tq].astype(o_ref.dtype)


def _dsa(q, qi, kw, ki4, k_bf, v_bf, *, tq, kb, causal, n_keys):
    s_total, t_q, qw = q.shape
    lp = ki4.shape[1]
    assert lp % kb == 0 and t_q % tq == 0 and kb % LANES == 0
    assert (kb % (2 * LANES) == 0) if causal else (lp == kb)
    kh = kb // 2 if causal else kb
    topk = min(TOPK_MAX, n_keys // 4)
    kernel = functools.partial(_dsa_kernel, tq=tq, kb=kb, topk=topk, causal=causal, n_keys=n_keys)
    qmap = lambda b, j: (b, j, 0)
    kmap = lambda b, j: (b, 0, 0)
    rows_g = HEADS_PER_KV * tq
    return pl.pallas_call(
        kernel,
        grid=(s_total, t_q // tq),
        in_specs=[
            pl.BlockSpec((1, tq, qw), qmap),
            pl.BlockSpec((1, tq, IDX_COLS), qmap),
            pl.BlockSpec((1, tq, LANES), qmap),
            pl.BlockSpec((1, lp, 2 * LANES), kmap, pipeline_mode=pl.Buffered(1)),
            pl.BlockSpec((1, lp, k_bf.shape[2]), kmap, pipeline_mode=pl.Buffered(1)),
            pl.BlockSpec((1, lp, v_bf.shape[2]), kmap, pipeline_mode=pl.Buffered(1)),
        ],
        out_specs=pl.BlockSpec((1, tq, qw), qmap),
        out_shape=jax.ShapeDtypeStruct((s_total, t_q, qw), BF16),
        scratch_shapes=[
            pltpu.VMEM((lp // kb, tq, kb), F32),
            pltpu.VMEM((N_IDX_HEADS * tq, 2 * LANES), BF16),
            pltpu.VMEM((N_IDX_HEADS, tq, LANES), F32),
            pltpu.VMEM((N_KV_HEADS, rows_g, HEAD_DIM), BF16),
            pltpu.VMEM((N_KV_HEADS, rows_g, HEAD_DIM), F32),
            pltpu.VMEM((N_KV_HEADS, rows_g, LANES), F32),
            pltpu.VMEM((N_KV_HEADS, rows_g, LANES), F32),
            pltpu.VMEM((tq, LANES), F32),
            pltpu.VMEM((tq, LANES), F32),
            pltpu.VMEM((tq, LANES), F32),
            pltpu.VMEM((tq, LANES), F32),
            pltpu.VMEM((tq, kb), F32),
            pltpu.VMEM((N_KV_HEADS, rows_g, kb), F32),
            pltpu.VMEM((N_KV_HEADS, rows_g, kb), BF16),
            pltpu.VMEM((N_KV_HEADS, rows_g, LANES), F32),
            pltpu.VMEM((2, N_IDX_HEADS * tq, kh), F32),
        ],
        compiler_params=_cparams(2),
        name="dsa",
    )(q, qi, kw, ki4, k_bf, v_bf)


def _dsa_prompt_kernel(q_ref, qi_ref, kw_ref, ki4_ref, k_ref, v_ref, o_ref,
                       scr, qi4_scr, wb_scr, qa_scr, ka_scr, bias_scr, acc_scr, m_scr, l_scr,
                       lg_scr, p_scr, corr_scr, s_scr, thr_scr, *, kb, topk):
    tq = LANES
    qb = pl.program_id(1)
    lane_row = lax.broadcasted_iota(jnp.int32, (1, tq), 1)
    n_adm = ((qb * tq + lane_row) // CHUNK + 1) * CHUNK
    nkb = ((qb + 1) * tq + kb - 1) // kb
    lane128 = lax.broadcasted_iota(jnp.int32, (tq, LANES), 1)
    sub128 = lax.broadcasted_iota(jnp.int32, (LANES, tq), 0)
    nt = (((1,), (1,)), ((), ()))
    rows_g = HEADS_PER_KV * tq

    qi = qi_ref[0]
    for h in range(N_IDX_HEADS):
        tile = qi[:, (h // 2) * LANES:(h // 2 + 1) * LANES]
        own = (lane128 < IDX_DIM) if h % 2 == 0 else (lane128 >= IDX_DIM)
        both = jnp.where(own, tile, pltpu.roll(tile, IDX_DIM, 1))
        hi = both.astype(BF16).astype(F32)
        mix = jnp.where(lane128 < IDX_DIM, hi, both - hi).astype(BF16)
        qi4_scr[h * tq:(h + 1) * tq, 0:LANES] = mix
        qi4_scr[h * tq:(h + 1) * tq, LANES:2 * LANES] = mix
        wb_scr[h] = jnp.broadcast_to(kw_ref[0, :, WI_LANE + h:WI_LANE + h + 1], (tq, LANES))
    eye = jnp.where(sub128 == lane128, 1.0, 0.0).astype(BF16)
    for g in range(N_KV_HEADS):
        for r in range(HEADS_PER_KV):
            hh = g * HEADS_PER_KV + r
            qa_scr[g, r * tq:(r + 1) * tq, 0:HEAD_DIM] = (
                q_ref[0, :, hh * HEAD_DIM:(hh + 1) * HEAD_DIM])
            qa_scr[g, r * tq:(r + 1) * tq, HEAD_DIM:HEAD_DIM + tq] = eye

    kh = kb // 2
    n_half = ki4_ref.shape[1] // kh

    def head_products(hidx, slot):
        kblk = ki4_ref[0, pl.ds(pl.multiple_of(hidx * kh, kh), kh), :]
        s_scr[slot] = lax.dot_general(qi4_scr[...], kblk, nt, preferred_element_type=F32)

    def reduce_heads(j, slot, row0, carry):
        rmin, rmax = carry
        for c in range(kh // LANES):
            cs = slice(c * LANES, (c + 1) * LANES)
            acc = jnp.zeros((tq, LANES), F32)
            for h in range(N_IDX_HEADS):
                acc = acc + wb_scr[h] * jnp.maximum(s_scr[slot, h * tq:(h + 1) * tq, cs], 0.0)
            acc_t = acc.T
            allowed = (j * kb + row0 + c * LANES + sub128) < n_adm
            scr[j, row0 + c * LANES:row0 + (c + 1) * LANES, :] = jnp.where(allowed, acc_t, NEG)
            rmin = jnp.minimum(rmin, jnp.where(allowed, acc_t, BIG))
            rmax = jnp.maximum(rmax, jnp.where(allowed, acc_t, -BIG))
        return rmin, rmax

    def score_step(j, carry):
        head_products(2 * j + 1, 1)
        carry = reduce_heads(j, 0, 0, carry)
        head_products(jnp.minimum(2 * j + 2, n_half - 1), 0)
        return reduce_heads(j, 1, kh, carry)

    head_products(0, 0)
    rmin, rmax = lax.fori_loop(
        0, nkb, score_step,
        (jnp.full((LANES, tq), BIG, F32), jnp.full((LANES, tq), -BIG, F32)))
    rmin = jnp.min(rmin, axis=0, keepdims=True)
    rmax = jnp.max(rmax, axis=0, keepdims=True)

    def over_tiles(tile_fn, init, combine):
        def body(j, acc):
            for c in range(kb // LANES):
                kpos = (j * kb + c * LANES + sub128).astype(F32)
                acc = combine(acc, tile_fn(scr[j, c * LANES:(c + 1) * LANES, :], kpos))
            return acc

        return lax.fori_loop(0, nkb, body, jnp.full((LANES, tq), init, F32))

    def fold_rows(x, combine):
        n = x.shape[0]
        while n > 8:
            n //= 2
            x = combine(x[:n], x[n:])
        return x

    def count(pred_fn):
        part = fold_rows(over_tiles(pred_fn, 0.0, jnp.add), jnp.add)
        return jnp.sum(part, axis=0, keepdims=True)

    def count_ge(t):
        return count(lambda s, kpos: jnp.where(s >= t, 1.0, 0.0))

    kf = np.float32(topk)
    need = jnp.where(n_adm > topk, 1.0, 0.0)

    def bis_cond(carry):
        return jnp.logical_and(carry[0] < BISECT_CAP, carry[1] > 0.0)

    def bis_body(carry):
        it, _, lo, hi, hit = carry
        mid = 0.5 * lo + 0.5 * hi
        cnt = count_ge(mid)
        new_lo = jnp.where(cnt >= kf, mid, lo)
        new_hi = jnp.where(cnt > kf, hi, mid)
        hit = jnp.where(cnt == kf, 1.0, hit)
        moved = jnp.where(new_lo != lo, 1.0, jnp.where(new_hi != hi, 1.0, 0.0))
        return it + 1, jnp.max(moved), new_lo, new_hi, hit

    _, _, lo, hi, hit = lax.while_loop(
        bis_cond, bis_body,
        (jnp.int32(0), jnp.float32(1.0), rmin, jnp.where(need > 0.5, rmax, rmin),
         jnp.zeros((1, tq), F32)))
    thr_scr[...] = lo
    tie_f = jnp.where(hit > 0.5, 0.0, need)

    @pl.when(jnp.max(tie_f) > 0.0)
    def _ties():
        tie = tie_f > 0.5

        def max_below(bound, strict):
            def tile(s, kpos):
                ok = (s < bound) if strict else (s <= bound)
                return jnp.where(ok, s, -BIG)

            part = fold_rows(over_tiles(tile, -BIG, jnp.maximum), jnp.maximum)
            return jnp.max(part, axis=0, keepdims=True)

        def short(c):
            return jnp.where(tie, jnp.where(c < kf, 1.0, 0.0), 0.0)

        def snap_cond(carry):
            return jnp.max(short(carry[1])) > 0.0

        def snap_body(carry):
            t_old, c_old = carry
            t_new = max_below(t_old, True)
            c_new = count_ge(t_new)
            upd = short(c_old) > 0.5
            return jnp.where(upd, t_new, t_old), jnp.where(upd, c_new, c_old)

        t0 = max_below(hi, False)
        t, _ = lax.while_loop(snap_cond, snap_body, (t0, count_ge(t0)))
        keep = kf - count(lambda s, kpos: jnp.where(s > t, 1.0, 0.0))

        def count_eq_before(cut):
            return count(lambda s, kpos: jnp.where(s == t, jnp.where(kpos < cut, 1.0, 0.0), 0.0))

        def cut_body(_, carry):
            clo, chi = carry
            cmid = jnp.floor(0.5 * (clo + chi))
            ok = count_eq_before(cmid) >= keep
            return jnp.where(ok, clo, cmid), jnp.where(ok, cmid, chi)

        n_total = np.float32(scr.shape[0] * kb)
        steps = int(np.ceil(np.log2(scr.shape[0] * kb))) + 1
        _, cut = lax.fori_loop(0, steps, cut_body, (jnp.zeros((1, tq), F32),
                                                    jnp.full((1, tq), n_total, F32)))
        cut = jnp.where(tie, cut, n_total)

        def drop(j, carry):
            for c in range(kb // LANES):
                rs = slice(c * LANES, (c + 1) * LANES)
                s = scr[j, rs, :]
                kpos = (j * kb + c * LANES + sub128).astype(F32)
                dropped = jnp.where(s == t, jnp.where(kpos >= cut, 1.0, 0.0), 0.0)
                scr[j, rs, :] = jnp.where(dropped > 0.5, NEG, s)
            return carry

        lax.fori_loop(0, nkb, drop, 0)
        thr_scr[...] = jnp.where(tie, t, lo)

    m_scr[...] = jnp.full(m_scr.shape, NEG, F32)
    l_scr[...] = jnp.zeros(l_scr.shape, F32)
    acc_scr[...] = jnp.zeros(acc_scr.shape, F32)
    thr = thr_scr[...]
    rt = ATT_ROW_TILE
    cols = [slice(c * LANES, (c + 1) * LANES) for c in range(kb // LANES)]

    def qk(j, g):
        koff = pl.ds(pl.multiple_of(j * kb, kb), kb)
        ka_scr[g, :, 0:HEAD_DIM] = k_ref[0, koff, g * HEAD_DIM:(g + 1) * HEAD_DIM]
        if g == 0:
            for rs in cols:
                mask = jnp.where(scr[j, rs, :] >= thr, 0.0, NEG).astype(BF16)
                ka_scr[0, rs, HEAD_DIM:HEAD_DIM + tq] = mask
                bias_scr[rs, :] = mask
        else:
            ka_scr[g, :, HEAD_DIM:HEAD_DIM + tq] = bias_scr[...]
        lg_scr[g] = lax.dot_general(qa_scr[g], ka_scr[g], nt, preferred_element_type=F32)

    def pv(j, g):
        koff = pl.ds(pl.multiple_of(j * kb, kb), kb)
        vg = v_ref[0, koff, g * HEAD_DIM:(g + 1) * HEAD_DIM]
        acc_scr[g] = acc_scr[g] * corr_scr[g] + jnp.dot(p_scr[g], vg, preferred_element_type=F32)

    def softmax(g):
        for i in range(rows_g // rt):
            rs = slice(i * rt, (i + 1) * rt)
            lg = [lg_scr[g, rs, cs] for cs in cols]
            blk_max = jnp.max(functools.reduce(jnp.maximum, lg), axis=1, keepdims=True)
            m_old = m_scr[g, rs, :]
            m_new = jnp.maximum(m_old, jnp.broadcast_to(blk_max, (rt, LANES)))
            corr = jnp.exp2(m_old - m_new)
            p_sum = jnp.zeros((rt, LANES), F32)
            for cs, lg_c in zip(cols, lg):
                p = jnp.exp2(lg_c - m_new)
                p_scr[g, rs, cs] = p.astype(BF16)
                p_sum = p_sum + p
            l_scr[g, rs, :] = corr * l_scr[g, rs, :] + p_sum
            corr_scr[g, rs, :] = corr
            m_scr[g, rs, :] = m_new

    def attend(j, carry):
        pv(jnp.maximum(j - 1, 0), 1)
        qk(j, 1)
        softmax(0)
        pv(j, 0)
        qk(jnp.minimum(j + 1, nkb - 1), 0)
        softmax(1)
        return carry

    p_scr[1] = jnp.zeros(p_scr.shape[1:], BF16)
    corr_scr[1] = jnp.zeros(corr_scr.shape[1:], F32)
    qk(0, 0)
    lax.fori_loop(0, nkb, attend, 0)
    pv(nkb - 1, 1)
    for g in range(N_KV_HEADS):
        o = acc_scr[g] / jnp.sum(l_scr[g], axis=1, keepdims=True)
        for r in range(HEADS_PER_KV):
            hh = g * HEADS_PER_KV + r
            o_ref[0, :, hh * HEAD_DIM:(hh + 1) * HEAD_DIM] = o[r * tq:(r + 1) * tq].astype(o_ref.dtype)


def _dsa_prompt(q, qi, kw, ki4, k_bf, v_bf, *, kb):
    s_total, t_q, qw = q.shape
    tq = LANES
    lp = ki4.shape[1]
    assert lp == t_q and t_q % tq == 0 and lp % kb == 0 and kb % (2 * LANES) == 0
    topk = min(TOPK_MAX, lp // 4)
    kernel = functools.partial(_dsa_prompt_kernel, kb=kb, topk=topk)
    qmap = lambda b, j: (b, j, 0)
    kmap = lambda b, j: (b, 0, 0)
    rows_g = HEADS_PER_KV * tq
    stat = pltpu.VMEM((N_KV_HEADS, rows_g, LANES), F32)
    return pl.pallas_call(
        kernel,
        grid=(s_total, t_q // tq),
        in_specs=[
            pl.BlockSpec((1, tq, qw), qmap),
            pl.BlockSpec((1, tq, IDX_COLS), qmap),
            pl.BlockSpec((1, tq, LANES), qmap),
            pl.BlockSpec((1, lp, 2 * LANES), kmap, pipeline_mode=pl.Buffered(1)),
            pl.BlockSpec((1, lp, k_bf.shape[2]), kmap, pipeline_mode=pl.Buffered(1)),
            pl.BlockSpec((1, lp, v_bf.shape[2]), kmap, pipeline_mode=pl.Buffered(1)),
        ],
        out_specs=pl.BlockSpec((1, tq, qw), qmap),
        out_shape=jax.ShapeDtypeStruct((s_total, t_q, qw), BF16),
        scratch_shapes=[
            pltpu.VMEM((lp // kb, kb, tq), F32),
            pltpu.VMEM((N_IDX_HEADS * tq, 2 * LANES), BF16),
            pltpu.VMEM((N_IDX_HEADS, tq, LANES), F32),
            pltpu.VMEM((N_KV_HEADS, rows_g, HEAD_DIM + tq), BF16),
            pltpu.VMEM((N_KV_HEADS, kb, HEAD_DIM + tq), BF16),
            pltpu.VMEM((kb, tq), BF16),
            pltpu.VMEM((N_KV_HEADS, rows_g, HEAD_DIM), F32),
            stat, stat,
            pltpu.VMEM((N_KV_HEADS, rows_g, kb), F32),
            pltpu.VMEM((N_KV_HEADS, rows_g, kb), BF16),
            stat,
            pltpu.VMEM((2, N_IDX_HEADS * tq, kb // 2), F32),
            pltpu.VMEM((1, tq), F32),
        ],
        compiler_params=_cparams(2),
        name="dsa_prompt",
    )(q, qi, kw, ki4, k_bf, v_bf)


def _mix_kernel(x_ref, gt_ref, gpost_ref, att_ref, u_ref, vn_ref, gates_ref, ws_ref, bs_ref,
                wpa_ref, wpb_ref, wout_ref, o_ref, sgu_scr):
    nb, t, d = x_ref.shape
    rows = nb * t
    gd = d // N_SGU_GROUPS
    dot = functools.partial(jnp.dot, preferred_element_type=F32)
    r = lax.broadcasted_iota(jnp.int32, (MLP_CHUNK, MLP_CHUNK), 0)
    c = lax.broadcasted_iota(jnp.int32, (MLP_CHUNK, MLP_CHUNK), 1)
    for g in range(N_SGU_GROUPS):
        wg = jnp.where(r >= c, ws_ref[g], 0.0).astype(BF16)
        gs = slice(g * gd, (g + 1) * gd)
        for ch in range(rows // MLP_CHUNK):
            rs = slice(ch * MLP_CHUNK, (ch + 1) * MLP_CHUNK)
            mixed = dot(wg, vn_ref[rs, gs].astype(BF16)) + bs_ref[:, gs]
            sgu_scr[rs, gs] = (u_ref[rs, gs].astype(F32) * mixed).astype(BF16)
    a = dot(att_ref[...], wpa_ref[...])
    b = dot(sgu_scr[...], wpb_ref[...])
    merged = gates_ref[:, 0:d].astype(F32) * a + gates_ref[:, d:2 * d].astype(F32) * b
    h = dot(merged.astype(BF16), wout_ref[...])
    hn = _rms(h, gpost_ref[...])
    o_ref[...] = x_ref[...] + gt_ref[...] * hn.reshape(nb, t, d)


def _row_blocking(s_total, t_full, nb):
    if nb == 1:
        t = min(t_full, 512)
        grid = (s_total, t_full // t)
        xmap = lambda b, j: (b, j, 0)
        mmap = lambda b, j: (b, 0, 0)
        rmap = lambda b, j: (b * (t_full // t) + j, 0)
    else:
        t = t_full
        grid = (s_total // nb,)
        xmap = lambda b: (b, 0, 0)
        mmap = lambda b: (b, 0, 0)
        rmap = lambda b: (b, 0)
    return t, grid, xmap, mmap, rmap


def _mix(x, gt, gpost, att, u, vn, gates, ws, bs, wpa, wpb, wout, nb):
    s_total, t_full, d = x.shape
    t, grid, xmap, mmap, rmap = _row_blocking(s_total, t_full, nb)
    rows = nb * t
    return pl.pallas_call(
        _mix_kernel,
        grid=grid,
        in_specs=[
            pl.BlockSpec((nb, t, d), xmap),
            pl.BlockSpec((nb, 1, d), mmap),
            _const_spec((1, 1, d)),
            pl.BlockSpec((rows, d), rmap),
            pl.BlockSpec((rows, d), rmap),
            pl.BlockSpec((rows, d), rmap),
            pl.BlockSpec((rows, 2 * d), rmap),
            _const_spec(ws.shape),
            _const_spec(bs.shape),
            _const_spec(wpa.shape),
            _const_spec(wpb.shape),
            _const_spec(wout.shape),
        ],
        out_specs=pl.BlockSpec((nb, t, d), xmap),
        out_shape=jax.ShapeDtypeStruct(x.shape, F32),
        scratch_shapes=[pltpu.VMEM((rows, d), BF16)],
        compiler_params=_cparams(len(grid)),
        name="mix",
    )(x, gt, gpost.reshape(1, 1, d), att, u, vn, gates, ws, bs, wpa, wpb, wout)


def _ffn_kernel(x_ref, sh_ref, sc_ref, gt_ref, gpre_ref, gpost_ref, win_ref, wout_ref, o_ref,
                *, n_split):
    nb, t, d = x_ref.shape
    rows = nb * t
    hidden = wout_ref.shape[0]
    hc = hidden // n_split
    dot = functools.partial(jnp.dot, preferred_element_type=F32)
    x = x_ref[...]
    xn = (_rms(x, gpre_ref[...]) * (1.0 + sc_ref[...]) + sh_ref[...]).reshape(rows, d).astype(BF16)
    acc = jnp.zeros((rows, d), F32)
    for i in range(n_split):
        gg = dot(xn, win_ref[:, i * hc:(i + 1) * hc])
        up = dot(xn, win_ref[:, hidden + i * hc:hidden + (i + 1) * hc])
        hmid = (gg * jax.nn.sigmoid(gg) * up).astype(BF16)
        acc = acc + dot(hmid, wout_ref[i * hc:(i + 1) * hc, :])
    hn = _rms(acc, gpost_ref[...])
    o_ref[...] = x + gt_ref[...] * hn.reshape(nb, t, d)


def _ffn(x, sh, sc, gt, gpre, gpost, win, wout, nb):
    s_total, t_full, d = x.shape
    t, grid, xmap, mmap, _ = _row_blocking(s_total, t_full, nb)
    hidden = wout.shape[0]
    n_split = 2
    assert hidden % (n_split * LANES) == 0
    return pl.pallas_call(
        functools.partial(_ffn_kernel, n_split=n_split),
        grid=grid,
        in_specs=[
            pl.BlockSpec((nb, t, d), xmap),
            pl.BlockSpec((nb, 1, d), mmap),
            pl.BlockSpec((nb, 1, d), mmap),
            pl.BlockSpec((nb, 1, d), mmap),
            _const_spec((1, 1, d)),
            _const_spec((1, 1, d)),
            _const_spec(win.shape),
            _const_spec(wout.shape),
        ],
        out_specs=pl.BlockSpec((nb, t, d), xmap),
        out_shape=jax.ShapeDtypeStruct(x.shape, F32),
        compiler_params=_cparams(len(grid)),
        name="ffn",
    )(x, sh, sc, gt, gpre.reshape(1, 1, d), gpost.reshape(1, 1, d), win, wout)


def _prep_weights(w_in, w_proj_a, w_proj_b, w_out, w_ffn_in, w_ffn_out, d):
    qw, kvw = N_HEADS * HEAD_DIM, N_KV_HEADS * HEAD_DIM
    o_q, o_k, o_v = 0, qw, qw + kvw
    o_qi = qw + 2 * kvw
    o_ki = o_qi + IDX_COLS
    o_wi = o_ki + IDX_DIM
    o_u = o_wi + N_IDX_HEADS
    wmain = jnp.concatenate([w_in[:, o_q:o_qi], w_in[:, o_u:]], axis=1).astype(BF16)
    pad = jnp.zeros((d, IDX_PAD - IDX_COLS - IDX_DIM - N_IDX_HEADS), F32)
    widx = jnp.concatenate([w_in[:, o_qi:o_u], pad], axis=1)
    wih, wil = _split_bf16(widx)
    return dict(
        wmain=wmain, wih=wih, wil=wil,
        wpa=w_proj_a.astype(BF16), wpb=w_proj_b.astype(BF16), wout=w_out.astype(BF16),
        wfi=w_ffn_in.astype(BF16), wfo=w_ffn_out.astype(BF16),
    )


def _spatial_tables(w_spatial, b_spatial, seq_len, d):
    cl = min(seq_len, MLP_CHUNK)
    reps = MLP_CHUNK // cl
    gd = d // N_SGU_GROUPS
    w = w_spatial[:, :cl, :cl]
    if reps > 1:
        eye = jnp.eye(reps, dtype=w.dtype)
        w = jnp.einsum("ab,gts->gatbs", eye, w).reshape(N_SGU_GROUPS, MLP_CHUNK, MLP_CHUNK)
    b = jnp.tile(b_spatial[:, :cl].T, (reps, 1))
    b = jnp.repeat(b, gd, axis=1)
    return w, b


def _stream(x, mod, pos, wts, g_pre_mix, g_post_mix, g_pre_ffn, g_post_ffn, b_gate, g_sgu,
            w_spatial, b_spatial, past, nb):
    s_total, t_full, d = x.shape
    sh1, sc1, gt1, sh2, sc2, gt2 = [m[:, None, :] for m in jnp.split(mod, 6, axis=-1)]

    idx_scale = np.float32((IDX_DIM * N_IDX_HEADS) ** -0.5)
    kw_scale = jnp.ones((LANES,), F32).at[WI_LANE:WI_LANE + N_IDX_HEADS].set(idx_scale)
    if nb == 1:
        tpos = pos
        tab_map = lambda j: (j, 0)
    else:
        tpos = jnp.tile(pos, nb)
        tab_map = lambda j: (0, 0)
    tabs = (_rope_tables(tpos, HEAD_DIM), _rope_tables(tpos, IDX_DIM), _kw_tables(tpos, kw_scale))

    vn_dtype = F32 if past is not None else BF16
    (q, k, v, k_bf, v_bf, qi, kw, ki4, u, vn, gates) = _proj(
        x, sh1, sc1, g_pre_mix, wts["wmain"], wts["wih"], wts["wil"], tabs, tab_map,
        b_gate, g_sgu, nb, vn_dtype)

    shp = lambda a: a.reshape(s_total, t_full, a.shape[-1])
    if past is None:
        n_keys = t_full
        att = _dsa_prompt(shp(q), shp(qi), shp(kw), shp(ki4), shp(k_bf), shp(v_bf),
                          kb=min(DSA_KEY_BLOCK, t_full))
    else:
        pk, pv, pki = past
        p_len = pk.shape[1]
        n_keys = p_len + t_full
        kb = LANES * ((n_keys + LANES - 1) // LANES)
        padn = kb - n_keys
        pkh, pkl = _split_bf16(pki)
        ki4_all = jnp.concatenate(
            [jnp.concatenate([pkh, pkh, pkl, pkl], axis=-1), shp(ki4)], axis=1)
        cat = lambda a, b_: jnp.pad(jnp.concatenate([a.astype(BF16), b_], axis=1),
                                    ((0, 0), (0, padn), (0, 0)))
        ki4_all = jnp.pad(ki4_all, ((0, 0), (0, padn), (0, 0)))
        att = _dsa(shp(q), shp(qi), shp(kw), ki4_all,
                   cat(pk.reshape(s_total, p_len, -1), shp(k_bf)),
                   cat(pv.reshape(s_total, p_len, -1), shp(v_bf)),
                   tq=t_full, kb=kb, causal=False, n_keys=n_keys)
    att = att.reshape(s_total * t_full, -1)

    ws, bs = _spatial_tables(w_spatial, b_spatial, t_full, d)
    x1 = _mix(x, gt1, g_post_mix, att, u, vn, gates, ws, bs,
              wts["wpa"], wts["wpb"], wts["wout"], nb)
    x2 = _ffn(x1, sh2, sc2, gt2, g_pre_ffn, g_post_ffn, wts["wfi"], wts["wfo"], nb)
    return x2, shp(k), shp(v), shp(kw)[..., :IDX_DIM], shp(vn)


def _kw_tables(pos, scale_lanes):
    tab = _rope_tables(pos, IDX_DIM)
    lane = jnp.arange(LANES)
    c = jnp.where(lane[None, :] < IDX_DIM, tab[0], 1.0) * scale_lanes[None, :]
    s = jnp.where(lane[None, :] < IDX_DIM, tab[1], 0.0)
    return jnp.stack([c, s])


def kernel(x_prompt, x_sample, cache_k, cache_v, cache_idx_k, c_prompt, c_sample, w_ada, b_ada,
           g_pre_mix, g_post_mix, g_pre_ffn, g_post_ffn, w_in, b_gate, g_sgu, w_spatial,
           b_spatial, w_proj_a, w_proj_b, w_out, w_ffn_in, w_ffn_out):
    depth = w_ada.shape[0]
    bp, t_p, d = x_prompt.shape
    bs_, t_s, _ = x_sample.shape
    past_len = cache_k.shape[2]
    pos_p = jnp.arange(t_p, dtype=jnp.int32)
    pos_s = past_len + jnp.arange(t_s, dtype=jnp.int32)
    nb_s = max(1, min(bs_, 256 // t_s))

    xp, xs = x_prompt, x_sample
    outs = [[] for _ in range(7)]
    for l in range(depth):
        wts = _prep_weights(w_in[l], w_proj_a[l], w_proj_b[l], w_out[l], w_ffn_in[l],
                            w_ffn_out[l], d)
        c_all = jnp.concatenate([c_prompt, c_sample], axis=0)
        pad_rows = (-c_all.shape[0]) % 8
        c_all = jnp.pad(c_all, ((0, pad_rows), (0, 0)))
        mod = _ada(c_all, w_ada[l], b_ada[l])
        common = (wts, g_pre_mix[l], g_post_mix[l], g_pre_ffn[l], g_post_ffn[l], b_gate[l],
                  g_sgu[l], w_spatial[l], b_spatial[l])
        xp, kp, vp, kip, _ = _stream(xp, mod[:bp], pos_p, *common, None, 1)
        past = (cache_k[l], cache_v[l], cache_idx_k[l])
        xs, ks, vs, kis, vns = _stream(xs, mod[bp:bp + bs_], pos_s, *common, past, nb_s)
        hd = (N_KV_HEADS, HEAD_DIM)
        for lst, val in zip(outs, (kp.reshape(bp, t_p, *hd), vp.reshape(bp, t_p, *hd), kip,
                                   ks.reshape(bs_, t_s, *hd), vs.reshape(bs_, t_s, *hd), kis,
                                   vns)):
            lst.append(val)
    return (xp, xs) + tuple(jnp.stack(o) for o in outs)
```

```python
import functools

import jax
import jax.numpy as jnp
import numpy as np
from jax import lax
from jax.experimental import pallas as pl
from jax.experimental.pallas import tpu as pltpu

F32 = jnp.float32
BF16 = jnp.bfloat16

CHUNK = 64
N_HEADS = 8
N_KV_HEADS = 2
HEAD_DIM = 128
N_IDX_HEADS = 8
IDX_DIM = 64
TOPK_MAX = 256
ROPE_THETA = 500000.0
ROT_FRACTION = 4
MLP_CHUNK = 128
N_SGU_GROUPS = 8
NEG = -1e30
BIG = 3e38
EPS = 1e-6

LANES = 128
VMEM_LIMIT_BYTES = 56 * 1024 * 1024

HEADS_PER_KV = N_HEADS // N_KV_HEADS
IDX_COLS = N_IDX_HEADS * IDX_DIM
IDX_PAD = IDX_COLS + LANES
WI_LANE = IDX_DIM
DSA_Q_BLOCK = 128
DSA_KEY_BLOCK = 512
ATT_ROW_TILE = 32
LOG2E = float(np.log2(np.e))
BISECT_CAP = 40


def _cparams(n_axes):
    return pltpu.CompilerParams(
        dimension_semantics=("arbitrary",) * n_axes,
        vmem_limit_bytes=VMEM_LIMIT_BYTES,
    )


def _const_spec(shape):
    nd = len(shape)
    return pl.BlockSpec(shape, lambda *_: (0,) * nd, pipeline_mode=pl.Buffered(1))


def _split_bf16(x):
    hi = x.astype(BF16)
    lo = (x - hi.astype(F32)).astype(BF16)
    return hi, lo


def _gelu(x):
    c = np.float32(np.sqrt(2.0 / np.pi))
    return 0.5 * x * (1.0 + jnp.tanh(c * (x + 0.044715 * (x * x * x))))


def _rms(x, g):
    return x * lax.rsqrt(jnp.mean(x * x, axis=-1, keepdims=True) + EPS) * g


def _ada_kernel(c_ref, w_ref, b_ref, o_ref):
    c = c_ref[...]
    a = c * jax.nn.sigmoid(c)
    ah, al = _split_bf16(a)
    wh, wl = _split_bf16(w_ref[...])
    dot = functools.partial(jnp.dot, preferred_element_type=F32)
    o_ref[...] = dot(ah, wh) + dot(al, wh) + dot(ah, wl) + b_ref[...]


def _ada(c, w_ada, b_ada):
    rows, d = c.shape
    n = w_ada.shape[1]
    tn = 1024
    return pl.pallas_call(
        _ada_kernel,
        grid=(n // tn,),
        in_specs=[
            pl.BlockSpec((rows, d), lambda j: (0, 0)),
            pl.BlockSpec((d, tn), lambda j: (0, j)),
            pl.BlockSpec((1, tn), lambda j: (0, j)),
        ],
        out_specs=pl.BlockSpec((rows, tn), lambda j: (0, j)),
        out_shape=jax.ShapeDtypeStruct((rows, n), F32),
        compiler_params=_cparams(1),
        name="ada",
    )(c, w_ada, b_ada.reshape(1, n))


def _rope_tables(pos, head_dim):
    rd = head_dim // ROT_FRACTION
    half = rd // 2
    inv = jnp.power(ROPE_THETA, -(jnp.arange(half, dtype=F32) / half))
    ang = pos.astype(F32)[:, None] * inv[None, :]
    cos, sin = jnp.cos(ang), jnp.sin(ang)
    t = pos.shape[0]
    ones = jnp.ones((t, head_dim - rd), F32)
    zeros = jnp.zeros((t, head_dim - rd), F32)
    c = jnp.concatenate([cos, cos, ones], axis=-1)
    s = jnp.concatenate([-sin, sin, zeros], axis=-1)
    reps = LANES // head_dim
    c, s = jnp.tile(c, (1, reps)), jnp.tile(s, (1, reps))
    return jnp.stack([c, s])


def _rope_apply(x, tab_ref, half):
    lane = lax.broadcasted_iota(jnp.int32, x.shape, 1)
    fwd = pltpu.roll(x, LANES - half, 1)
    bwd = pltpu.roll(x, half, 1)
    partner = jnp.where((lane % (2 * half)) < half, fwd, bwd)
    return x * tab_ref[0] + partner * tab_ref[1]


def _proj_kernel(x_ref, sh_ref, sc_ref, gpre_ref, wmain_ref, wih_ref, wil_ref,
                 tq_ref, ti_ref, tk_ref, bgate_ref, gsgu_ref,
                 q_ref, k_ref, v_ref, kbf_ref, vbf_ref, qi_ref, kw_ref, ki4_ref,
                 u_ref, vn_ref, gates_ref):
    nb, t, d = x_ref.shape
    rows = nb * t
    xn = _rms(x_ref[...], gpre_ref[...]) * (1.0 + sc_ref[...]) + sh_ref[...]
    xn = xn.reshape(rows, d)
    xh, xl = _split_bf16(xn)
    dot = functools.partial(jnp.dot, preferred_element_type=F32)
    q_scale = np.float32(HEAD_DIM ** -0.5 * LOG2E)
    qw = N_HEADS * HEAD_DIM
    kvw = N_KV_HEADS * HEAD_DIM
    half_h = HEAD_DIM // ROT_FRACTION // 2
    half_i = IDX_DIM // ROT_FRACTION // 2

    y = dot(xh, wmain_ref[:, 0:qw])
    for h in range(N_HEADS):
        sl = slice(h * HEAD_DIM, (h + 1) * HEAD_DIM)
        q_ref[:, sl] = (_rope_apply(y[:, sl], tq_ref, half_h) * q_scale).astype(BF16)

    y = dot(xh, wmain_ref[:, qw:qw + kvw])
    for h in range(N_KV_HEADS):
        sl = slice(h * HEAD_DIM, (h + 1) * HEAD_DIM)
        kr = _rope_apply(y[:, sl], tq_ref, half_h)
        k_ref[pl.ds(h, rows, stride=N_KV_HEADS), :] = kr
        kbf_ref[:, sl] = kr.astype(BF16)
    y = dot(xh, wmain_ref[:, qw + kvw:qw + 2 * kvw])
    for h in range(N_KV_HEADS):
        v_ref[pl.ds(h, rows, stride=N_KV_HEADS), :] = y[:, h * HEAD_DIM:(h + 1) * HEAD_DIM]
    vbf_ref[...] = y.astype(BF16)

    wih = wih_ref[...]
    yi = dot(xh, wih) + dot(xl, wih) + dot(xh, wil_ref[...])
    for tile in range(IDX_COLS // LANES):
        sl = slice(tile * LANES, (tile + 1) * LANES)
        qi_ref[:, sl] = _rope_apply(yi[:, sl], ti_ref, half_i)
    kw = _rope_apply(yi[:, IDX_COLS:IDX_PAD], tk_ref, half_i)
    kw_ref[...] = kw
    lane = lax.broadcasted_iota(jnp.int32, kw.shape, 1)
    kk = jnp.where(lane < IDX_DIM, kw, pltpu.roll(kw, IDX_DIM, 1))
    kh, kl = _split_bf16(kk)
    ki4_ref[:, 0:LANES] = kh
    ki4_ref[:, LANES:2 * LANES] = kl

    base = qw + 2 * kvw
    u_ref[...] = _gelu(dot(xh, wmain_ref[:, base:base + d])).astype(u_ref.dtype)
    g = _gelu(dot(xh, wmain_ref[:, base + d:base + 2 * d]))
    mu = jnp.mean(g, axis=-1, keepdims=True)
    gc = g - mu
    vn = gc * lax.rsqrt(jnp.mean(gc * gc, axis=-1, keepdims=True) + EPS) * gsgu_ref[...]
    vn_ref[...] = vn.astype(vn_ref.dtype)

    y = dot(xh, wmain_ref[:, base + 2 * d:base + 4 * d]) + bgate_ref[...]
    gates_ref[...] = jax.nn.sigmoid(y).astype(BF16)


def _proj(x, sh, sc, gpre, wmain, wih, wil, tabs, tab_map, bgate, gsgu, nb, vn_dtype):
    s_total, t_full, d = x.shape
    tq, ti, tk = tabs
    if nb == 1:
        t = min(t_full, 512)
        grid = (s_total, t_full // t)
        xmap = lambda b, j: (b, j, 0)
        mmap = lambda b, j: (b, 0, 0)
        rmap = lambda b, j: (b * (t_full // t) + j, 0)
        tmap = lambda b, j: (0,) + tab_map(j)
    else:
        t = t_full
        grid = (s_total // nb,)
        xmap = lambda b: (b, 0, 0)
        mmap = lambda b: (b, 0, 0)
        rmap = lambda b: (b, 0)
        tmap = lambda b: (0,) + tab_map(0)
    rows = nb * t
    n_rows = s_total * t_full
    qw, kvw = N_HEADS * HEAD_DIM, N_KV_HEADS * HEAD_DIM

    def row_spec(width, per_token=1):
        return pl.BlockSpec((rows * per_token, width), rmap)

    def row_shape(width, dtype, per_token=1):
        return jax.ShapeDtypeStruct((n_rows * per_token, width), dtype)

    tab_spec = pl.BlockSpec((2, rows, LANES), tmap)
    outs = [(qw, BF16), (HEAD_DIM, F32, N_KV_HEADS), (HEAD_DIM, F32, N_KV_HEADS), (kvw, BF16),
            (kvw, BF16), (IDX_COLS, F32), (LANES, F32), (2 * LANES, BF16), (d, BF16),
            (d, vn_dtype), (2 * d, BF16)]
    return pl.pallas_call(
        _proj_kernel,
        grid=grid,
        in_specs=[
            pl.BlockSpec((nb, t, d), xmap),
            pl.BlockSpec((nb, 1, d), mmap),
            pl.BlockSpec((nb, 1, d), mmap),
            _const_spec((1, 1, d)),
            _const_spec(wmain.shape),
            _const_spec(wih.shape),
            _const_spec(wil.shape),
            tab_spec, tab_spec, tab_spec,
            _const_spec((1, 2 * d)),
            _const_spec((1, d)),
        ],
        out_specs=[row_spec(o[0], *o[2:]) for o in outs],
        out_shape=[row_shape(*o) for o in outs],
        compiler_params=_cparams(len(grid)),
        name="proj",
    )(x, sh, sc, gpre.reshape(1, 1, d), wmain, wih, wil, tq, ti, tk,
      bgate.reshape(1, 2 * d), gsgu.reshape(1, d))


def _dsa_sample_kernel(q_ref, qi_ref, kw_ref, kic_ref, ki4n_ref, kc_ref, kn_ref, vc_ref, vn_ref,
                       o_ref, scr, qi4_scr, wb_scr, qg_scr, acc_scr, m_scr, l_scr, thr_scr, lo_scr,
                       hi_scr, hit_scr, bias_scr, lg_scr, p_scr, corr_scr, s_scr, ki4_scr, k_scr,
                       v_scr, *, tq, kb, topk, n_keys):
    n_past = kic_ref.shape[1]
    n_adm = jnp.full((tq, 1), n_keys, jnp.int32)
    nkb = 1
    lane128 = lax.broadcasted_iota(jnp.int32, (tq, LANES), 1)
    nt = (((1,), (1,)), ((), ()))
    dot = functools.partial(jnp.dot, preferred_element_type=F32)

    sub = lax.broadcasted_iota(jnp.int32, (IDX_DIM, LANES), 0)
    lane = lax.broadcasted_iota(jnp.int32, (IDX_DIM, LANES), 1)
    dup = jnp.where(sub == lane % IDX_DIM, 1.0, 0.0).astype(BF16)
    kih, kil = _split_bf16(kic_ref[0])
    ki4_scr[0:n_past, 0:LANES] = dot(kih, dup).astype(BF16)
    ki4_scr[0:n_past, LANES:2 * LANES] = dot(kil, dup).astype(BF16)
    ki4_scr[n_past:n_keys, :] = ki4n_ref[0]
    ki4_scr[n_keys:kb, :] = jnp.zeros((kb - n_keys, 2 * LANES), BF16)
    for g in range(N_KV_HEADS):
        for dst, cache, new in ((k_scr, kc_ref, kn_ref), (v_scr, vc_ref, vn_ref)):
            dst[g, 0:n_past, :] = cache[0, pl.ds(g, n_past, stride=N_KV_HEADS), :].astype(BF16)
            dst[g, n_past:n_keys, :] = new[0, pl.ds(g, tq, stride=N_KV_HEADS), :].astype(BF16)
            dst[g, n_keys:kb, :] = jnp.zeros((kb - n_keys, HEAD_DIM), BF16)

    qi = qi_ref[0]
    for h in range(N_IDX_HEADS):
        tile = qi[:, (h // 2) * LANES:(h // 2 + 1) * LANES]
        own = (lane128 < IDX_DIM) if h % 2 == 0 else (lane128 >= IDX_DIM)
        both = jnp.where(own, tile, pltpu.roll(tile, IDX_DIM, 1))
        hi = both.astype(BF16).astype(F32)
        mix = jnp.where(lane128 < IDX_DIM, hi, both - hi).astype(BF16)
        qi4_scr[h * tq:(h + 1) * tq, 0:LANES] = mix
        qi4_scr[h * tq:(h + 1) * tq, LANES:2 * LANES] = mix
        wb_scr[h] = jnp.broadcast_to(kw_ref[0, :, WI_LANE + h:WI_LANE + h + 1], (tq, LANES))
    for g in range(N_KV_HEADS):
        for r in range(HEADS_PER_KV):
            hh = g * HEADS_PER_KV + r
            qg_scr[g, r * tq:(r + 1) * tq, :] = q_ref[0, :, hh * HEAD_DIM:(hh + 1) * HEAD_DIM]

    s_scr[...] = lax.dot_general(qi4_scr[...], ki4_scr[...], nt, preferred_element_type=F32)
    rmin = jnp.full((tq, LANES), BIG, F32)
    rmax = jnp.full((tq, LANES), -BIG, F32)
    for c in range(kb // LANES):
        cs = slice(c * LANES, (c + 1) * LANES)
        acc = jnp.zeros((tq, LANES), F32)
        for h in range(N_IDX_HEADS):
            acc = acc + wb_scr[h] * jnp.maximum(s_scr[h * tq:(h + 1) * tq, cs], 0.0)
        allowed = (c * LANES + lane128) < n_adm
        scr[0, :, cs] = jnp.where(allowed, acc, NEG)
        rmin = jnp.minimum(rmin, jnp.where(allowed, acc, BIG))
        rmax = jnp.maximum(rmax, jnp.where(allowed, acc, -BIG))
    rmin = jnp.min(rmin, axis=1, keepdims=True)
    rmax = jnp.max(rmax, axis=1, keepdims=True)

    def rep(col):
        return jnp.broadcast_to(col, (tq, LANES))

    def over_tiles(tile_fn, init, combine):
        def body(j, acc):
            for c in range(kb // LANES):
                kpos = (j * kb + c * LANES + lane128).astype(F32)
                acc = combine(acc, tile_fn(scr[j, :, c * LANES:(c + 1) * LANES], kpos))
            return acc

        return lax.fori_loop(0, nkb, body, jnp.full((tq, LANES), init, F32))

    def count(pred_fn):
        acc = over_tiles(lambda s, kpos: pred_fn(s, kpos), 0.0, jnp.add)
        return rep(jnp.sum(acc, axis=1, keepdims=True))

    def count_ge(t):
        return count(lambda s, kpos: jnp.where(s >= t, 1.0, 0.0))

    kf = np.float32(topk)
    rmin, rmax = rep(rmin), rep(rmax)
    need = rep(jnp.where(n_adm > topk, 1.0, 0.0))
    lo_scr[...] = rmin
    hi_scr[...] = jnp.where(need > 0.5, rmax, rmin)
    hit_scr[...] = jnp.zeros((tq, LANES), F32)

    def bis_cond(carry):
        it, n_moved = carry
        return jnp.logical_and(it < BISECT_CAP, n_moved > 0.0)

    def bis_body(carry):
        it, _ = carry
        lo, hi = lo_scr[...], hi_scr[...]
        mid = 0.5 * lo + 0.5 * hi
        cnt = count_ge(mid)
        new_lo = jnp.where(cnt >= kf, mid, lo)
        new_hi = jnp.where(cnt > kf, hi, mid)
        lo_scr[...] = new_lo
        hi_scr[...] = new_hi
        hit_scr[...] = jnp.where(cnt == kf, 1.0, hit_scr[...])
        moved = jnp.where(new_lo != lo, 1.0, jnp.where(new_hi != hi, 1.0, 0.0))
        return it + 1, jnp.sum(moved)

    lax.while_loop(bis_cond, bis_body, (jnp.int32(0), jnp.float32(1.0)))
    thr_scr[...] = lo_scr[...]
    tie_f = jnp.where(hit_scr[...] > 0.5, 0.0, need)

    @pl.when(jnp.sum(tie_f) > 0.0)
    def _ties():
        tie = tie_f > 0.5

        def max_below(bound, strict):
            def tile(s, kpos):
                ok = (s < bound) if strict else (s <= bound)
                return jnp.where(ok, s, -BIG)

            return rep(jnp.max(over_tiles(tile, -BIG, jnp.maximum), axis=1, keepdims=True))

        def short(c):
            return jnp.logical_and(tie, c < kf)

        def snap_cond(carry):
            return jnp.sum(jnp.where(short(carry[1]), 1.0, 0.0)) > 0.0

        def snap_body(carry):
            t_old, c_old = carry
            t_new = max_below(t_old, True)
            c_new = count_ge(t_new)
            upd = short(c_old)
            return jnp.where(upd, t_new, t_old), jnp.where(upd, c_new, c_old)

        t0 = max_below(hi_scr[...], False)
        t, _ = lax.while_loop(snap_cond, snap_body, (t0, count_ge(t0)))
        keep = kf - count(lambda s, kpos: jnp.where(s > t, 1.0, 0.0))

        def count_eq_before(cut):
            return count(lambda s, kpos: jnp.where(s == t, jnp.where(kpos < cut, 1.0, 0.0), 0.0))

        def cut_body(_, carry):
            clo, chi = carry
            cmid = jnp.floor(0.5 * (clo + chi))
            ok = count_eq_before(cmid) >= keep
            return jnp.where(ok, clo, cmid), jnp.where(ok, cmid, chi)

        n_total = np.float32(scr.shape[0] * kb)
        steps = int(np.ceil(np.log2(scr.shape[0] * kb))) + 1
        _, cut = lax.fori_loop(0, steps, cut_body, (jnp.zeros((tq, LANES), F32),
                                                    jnp.full((tq, LANES), n_total, F32)))
        cut = jnp.where(tie, cut, n_total)

        def drop(j, carry):
            for c in range(kb // LANES):
                cs = slice(c * LANES, (c + 1) * LANES)
                s = scr[j, :, cs]
                kpos = (j * kb + c * LANES + lane128).astype(F32)
                dropped = jnp.where(s == t, jnp.where(kpos >= cut, 1.0, 0.0), 0.0)
                scr[j, :, cs] = jnp.where(dropped > 0.5, NEG, s)
            return carry

        lax.fori_loop(0, nkb, drop, 0)
        thr_scr[...] = jnp.where(tie, t, thr_scr[...])

    m_scr[...] = jnp.full(m_scr.shape, NEG, F32)
    l_scr[...] = jnp.zeros(l_scr.shape, F32)
    acc_scr[...] = jnp.zeros(acc_scr.shape, F32)
    thr = thr_scr[...]

    rows_g = HEADS_PER_KV * tq
    rt = min(tq, ATT_ROW_TILE)
    cols = [slice(c * LANES, (c + 1) * LANES) for c in range(kb // LANES)]

    def qk(g):
        lg_scr[g] = lax.dot_general(qg_scr[g], k_scr[g], nt, preferred_element_type=F32)

    def pv(g):
        acc_scr[g] = acc_scr[g] * corr_scr[g] + dot(p_scr[g], v_scr[g])

    def softmax(g):
        for i in range(rows_g // rt):
            rs = slice(i * rt, (i + 1) * rt)
            qs = slice((i * rt) % tq, (i * rt) % tq + rt)
            lg = [lg_scr[g, rs, cs] + bias_scr[qs, cs] for cs in cols]
            blk_max = jnp.max(functools.reduce(jnp.maximum, lg), axis=1, keepdims=True)
            m_old = m_scr[g, rs, :]
            m_new = jnp.maximum(m_old, jnp.broadcast_to(blk_max, (rt, LANES)))
            corr = jnp.exp2(m_old - m_new)
            p_sum = jnp.zeros((rt, LANES), F32)
            for cs, lg_c in zip(cols, lg):
                p = jnp.exp2(lg_c - m_new)
                p_scr[g, rs, cs] = p.astype(BF16)
                p_sum = p_sum + p
            l_scr[g, rs, :] = corr * l_scr[g, rs, :] + p_sum
            corr_scr[g, rs, :] = corr
            m_scr[g, rs, :] = m_new

    def select_bias(j):
        for cs in cols:
            bias_scr[:, cs] = jnp.where(scr[j, :, cs] >= thr, 0.0, NEG)

    qk(0)
    qk(1)
    select_bias(0)
    softmax(0)
    pv(0)
    softmax(1)
    pv(1)
    for g in range(N_KV_HEADS):
        o = acc_scr[g] / jnp.sum(l_scr[g], axis=1, keepdims=True)
        for r in range(HEADS_PER_KV):
            hh = g * HEADS_PER_KV + r
            o_ref[0, :, hh * HEAD_DIM:(hh + 1) * HEAD_DIM] = o[r * tq:(r + 1) * tq].astype(o_ref.dtype)


def _dsa_sample(q, qi, kw, kic, ki4n, kc, kn, vc, vn):
    s_total, tq, qw = q.shape
    n_past = kic.shape[1]
    n_keys = n_past + tq
    kb = LANES * ((n_keys + LANES - 1) // LANES)
    topk = min(TOPK_MAX, n_keys // 4)
    kernel = functools.partial(_dsa_sample_kernel, tq=tq, kb=kb, topk=topk, n_keys=n_keys)
    bmap = lambda b: (b, 0, 0)
    rows_g = HEADS_PER_KV * tq
    return pl.pallas_call(
        kernel,
        grid=(s_total,),
        in_specs=[
            pl.BlockSpec((1, tq, qw), bmap),
            pl.BlockSpec((1, tq, IDX_COLS), bmap),
            pl.BlockSpec((1, tq, LANES), bmap),
            pl.BlockSpec((1, n_past, IDX_DIM), bmap),
            pl.BlockSpec((1, tq, 2 * LANES), bmap),
            pl.BlockSpec((1, N_KV_HEADS * n_past, HEAD_DIM), bmap),
            pl.BlockSpec((1, N_KV_HEADS * tq, HEAD_DIM), bmap),
            pl.BlockSpec((1, N_KV_HEADS * n_past, HEAD_DIM), bmap),
            pl.BlockSpec((1, N_KV_HEADS * tq, HEAD_DIM), bmap),
        ],
        out_specs=pl.BlockSpec((1, tq, qw), bmap),
        out_shape=jax.ShapeDtypeStruct((s_total, tq, qw), BF16),
        scratch_shapes=[
            pltpu.VMEM((1, tq, kb), F32),
            pltpu.VMEM((N_IDX_HEADS * tq, 2 * LANES), BF16),
            pltpu.VMEM((N_IDX_HEADS, tq, LANES), F32),
            pltpu.VMEM((N_KV_HEADS, rows_g, HEAD_DIM), BF16),
            pltpu.VMEM((N_KV_HEADS, rows_g, HEAD_DIM), F32),
            pltpu.VMEM((N_KV_HEADS, rows_g, LANES), F32),
            pltpu.VMEM((N_KV_HEADS, rows_g, LANES), F32),
            pltpu.VMEM((tq, LANES), F32),
            pltpu.VMEM((tq, LANES), F32),
            pltpu.VMEM((tq, LANES), F32),
            pltpu.VMEM((tq, LANES), F32),
            pltpu.VMEM((tq, kb), F32),
            pltpu.VMEM((N_KV_HEADS, rows_g, kb), F32),
            pltpu.VMEM((N_KV_HEADS, rows_g, kb), BF16),
            pltpu.VMEM((N_KV_HEADS, rows_g, LANES), F32),
            pltpu.VMEM((N_IDX_HEADS * tq, kb), F32),
            pltpu.VMEM((kb, 2 * LANES), BF16),
            pltpu.VMEM((N_KV_HEADS, kb, HEAD_DIM), BF16),
            pltpu.VMEM((N_KV_HEADS, kb, HEAD_DIM), BF16),
        ],
        compiler_params=_cparams(1),
        name="dsa_sample",
    )(q, qi, kw, kic, ki4n, kc, kn, vc, vn)


def _dsa_prompt_kernel(q_ref, qi_ref, kw_ref, ki4_ref, k_ref, v_ref, o_ref,
                       scr, qi4_scr, wb_scr, qa_scr, ka_scr, bias_scr, acc_scr, m_scr, l_scr,
                       lg_scr, p_scr, corr_scr, s_scr, thr_scr, *, kb, topk):
    tq = LANES
    qb = pl.program_id(1)
    lane_row = lax.broadcasted_iota(jnp.int32, (1, tq), 1)
    n_adm = ((qb * tq + lane_row) // CHUNK + 1) * CHUNK
    nkb = ((qb + 1) * tq + kb - 1) // kb
    lane128 = lax.broadcasted_iota(jnp.int32, (tq, LANES), 1)
    sub128 = lax.broadcasted_iota(jnp.int32, (LANES, tq), 0)
    nt = (((1,), (1,)), ((), ()))
    rows_g = HEADS_PER_KV * tq

    qi = qi_ref[0]
    for h in range(N_IDX_HEADS):
        tile = qi[:, (h // 2) * LANES:(h // 2 + 1) * LANES]
        own = (lane128 < IDX_DIM) if h % 2 == 0 else (lane128 >= IDX_DIM)
        both = jnp.where(own, tile, pltpu.roll(tile, IDX_DIM, 1))
        hi = both.astype(BF16).astype(F32)
        mix = jnp.where(lane128 < IDX_DIM, hi, both - hi).astype(BF16)
        qi4_scr[h * tq:(h + 1) * tq, 0:LANES] = mix
        qi4_scr[h * tq:(h + 1) * tq, LANES:2 * LANES] = mix
        wb_scr[h] = jnp.broadcast_to(kw_ref[0, :, WI_LANE + h:WI_LANE + h + 1], (tq, LANES))
    eye = jnp.where(sub128 == lane128, 1.0, 0.0).astype(BF16)
    for g in range(N_KV_HEADS):
        for r in range(HEADS_PER_KV):
            hh = g * HEADS_PER_KV + r
            qa_scr[g, r * tq:(r + 1) * tq, 0:HEAD_DIM] = (
                q_ref[0, :, hh * HEAD_DIM:(hh + 1) * HEAD_DIM])
            qa_scr[g, r * tq:(r + 1) * tq, HEAD_DIM:HEAD_DIM + tq] = eye

    kh = kb // 2
    n_half = ki4_ref.shape[1] // kh

    def head_products(hidx, slot):
        kblk = ki4_ref[0, pl.ds(pl.multiple_of(hidx * kh, kh), kh), :]
        s_scr[slot] = lax.dot_general(qi4_scr[...], kblk, nt, preferred_element_type=F32)

    def reduce_heads(j, slot, row0, carry):
        rmin, rmax = carry
        for c in range(kh // LANES):
            cs = slice(c * LANES, (c + 1) * LANES)
            acc = jnp.zeros((tq, LANES), F32)
            for h in range(N_IDX_HEADS):
                acc = acc + wb_scr[h] * jnp.maximum(s_scr[slot, h * tq:(h + 1) * tq, cs], 0.0)
            acc_t = acc.T
            allowed = (j * kb + row0 + c * LANES + sub128) < n_adm
            scr[j, row0 + c * LANES:row0 + (c + 1) * LANES, :] = jnp.where(allowed, acc_t, NEG)
            rmin = jnp.minimum(rmin, jnp.where(allowed, acc_t, BIG))
            rmax = jnp.maximum(rmax, jnp.where(allowed, acc_t, -BIG))
        return rmin, rmax

    def score_step(j, carry):
        head_products(2 * j + 1, 1)
        carry = reduce_heads(j, 0, 0, carry)
        head_products(jnp.minimum(2 * j + 2, n_half - 1), 0)
        return reduce_heads(j, 1, kh, carry)

    head_products(0, 0)
    rmin, rmax = lax.fori_loop(
        0, nkb, score_step,
        (jnp.full((LANES, tq), BIG, F32), jnp.full((LANES, tq), -BIG, F32)))
    rmin = jnp.min(rmin, axis=0, keepdims=True)
    rmax = jnp.max(rmax, axis=0, keepdims=True)

    def over_tiles(tile_fn, init, combine):
        def body(j, acc):
            for c in range(kb // LANES):
                kpos = (j * kb + c * LANES + sub128).astype(F32)
                acc = combine(acc, tile_fn(scr[j, c * LANES:(c + 1) * LANES, :], kpos))
            return acc

        return lax.fori_loop(0, nkb, body, jnp.full((LANES, tq), init, F32))

    def fold_rows(x, combine):
        n = x.shape[0]
        while n > 8:
            n //= 2
            x = combine(x[:n], x[n:])
        return x

    def count(pred_fn):
        part = fold_rows(over_tiles(pred_fn, 0.0, jnp.add), jnp.add)
        return jnp.sum(part, axis=0, keepdims=True)

    def count_ge(t):
        return count(lambda s, kpos: jnp.where(s >= t, 1.0, 0.0))

    kf = np.float32(topk)
    need = jnp.where(n_adm > topk, 1.0, 0.0)

    def bis_cond(carry):
        return jnp.logical_and(carry[0] < BISECT_CAP, carry[1] > 0.0)

    def bisect_pass(lo, hi, hit):
        mid = 0.5 * lo + 0.5 * hi
        cnt = count_ge(mid)
        new_lo = jnp.where(cnt >= kf, mid, lo)
        new_hi = jnp.where(cnt > kf, hi, mid)
        hit = jnp.where(cnt == kf, 1.0, hit)
        moved = jnp.where(new_lo != lo, 1.0, jnp.where(new_hi != hi, 1.0, 0.0))
        return new_lo, new_hi, hit, jnp.where(new_lo < new_hi, moved, 0.0)

    def bis_body(carry):
        it, _, lo, hi, hit = carry
        lo, hi, hit, _ = bisect_pass(lo, hi, hit)
        lo, hi, hit, pending = bisect_pass(lo, hi, hit)
        return it + 2, jnp.max(pending), lo, hi, hit

    _, _, lo, hi, hit = lax.while_loop(
        bis_cond, bis_body,
        (jnp.int32(0), jnp.float32(1.0), rmin, jnp.where(need > 0.5, rmax, rmin),
         jnp.zeros((1, tq), F32)))
    thr_scr[...] = lo
    tie_f = jnp.where(hit > 0.5, 0.0, need)

    @pl.when(jnp.max(tie_f) > 0.0)
    def _ties():
        tie = tie_f > 0.5

        def max_below(bound, strict):
            def tile(s, kpos):
                ok = (s < bound) if strict else (s <= bound)
                return jnp.where(ok, s, -BIG)

            part = fold_rows(over_tiles(tile, -BIG, jnp.maximum), jnp.maximum)
            return jnp.max(part, axis=0, keepdims=True)

        def short(c):
            return jnp.where(tie, jnp.where(c < kf, 1.0, 0.0), 0.0)

        def snap_cond(carry):
            return jnp.max(short(carry[1])) > 0.0

        def snap_body(carry):
            t_old, c_old = carry
            t_new = max_below(t_old, True)
            c_new = count_ge(t_new)
            upd = short(c_old) > 0.5
            return jnp.where(upd, t_new, t_old), jnp.where(upd, c_new, c_old)

        t0 = max_below(hi, False)
        t, _ = lax.while_loop(snap_cond, snap_body, (t0, count_ge(t0)))
        keep = kf - count(lambda s, kpos: jnp.where(s > t, 1.0, 0.0))

        def count_eq_before(cut):
            return count(lambda s, kpos: jnp.where(s == t, jnp.where(kpos < cut, 1.0, 0.0), 0.0))

        def cut_body(_, carry):
            clo, chi = carry
            cmid = jnp.floor(0.5 * (clo + chi))
            ok = count_eq_before(cmid) >= keep
            return jnp.where(ok, clo, cmid), jnp.where(ok, cmid, chi)

        n_total = np.float32(scr.shape[0] * kb)
        steps = int(np.ceil(np.log2(scr.shape[0] * kb))) + 1
        _, cut = lax.fori_loop(0, steps, cut_body, (jnp.zeros((1, tq), F32),
                                                    jnp.full((1, tq), n_total, F32)))
        cut = jnp.where(tie, cut, n_total)

        def drop(j, carry):
            for c in range(kb // LANES):
                rs = slice(c * LANES, (c + 1) * LANES)
                s = scr[j, rs, :]
                kpos = (j * kb + c * LANES + sub128).astype(F32)
                dropped = jnp.where(s == t, jnp.where(kpos >= cut, 1.0, 0.0), 0.0)
                scr[j, rs, :] = jnp.where(dropped > 0.5, NEG, s)
            return carry

        lax.fori_loop(0, nkb, drop, 0)
        thr_scr[...] = jnp.where(tie, t, lo)

    m_scr[...] = jnp.full(m_scr.shape, NEG, F32)
    l_scr[...] = jnp.zeros(l_scr.shape, F32)
    acc_scr[...] = jnp.zeros(acc_scr.shape, F32)
    thr = thr_scr[...]
    rt = ATT_ROW_TILE
    cols = [slice(c * LANES, (c + 1) * LANES) for c in range(kb // LANES)]

    def qk(j, g):
        koff = pl.ds(pl.multiple_of(j * kb, kb), kb)
        ka_scr[g, :, 0:HEAD_DIM] = k_ref[0, koff, g * HEAD_DIM:(g + 1) * HEAD_DIM]
        if g == 0:
            for rs in cols:
                mask = jnp.where(scr[j, rs, :] >= thr, 0.0, NEG).astype(BF16)
                ka_scr[0, rs, HEAD_DIM:HEAD_DIM + tq] = mask
                bias_scr[rs, :] = mask
        else:
            ka_scr[g, :, HEAD_DIM:HEAD_DIM + tq] = bias_scr[...]
        lg_scr[g] = lax.dot_general(qa_scr[g], ka_scr[g], nt, preferred_element_type=F32)

    def pv(j, g):
        koff = pl.ds(pl.multiple_of(j * kb, kb), kb)
        vg = v_ref[0, koff, g * HEAD_DIM:(g + 1) * HEAD_DIM]
        acc_scr[g] = acc_scr[g] * corr_scr[g] + jnp.dot(p_scr[g], vg, preferred_element_type=F32)

    def softmax(g):
        for i in range(rows_g // rt):
            rs = slice(i * rt, (i + 1) * rt)
            lg = [lg_scr[g, rs, cs] for cs in cols]
            blk_max = jnp.max(functools.reduce(jnp.maximum, lg), axis=1, keepdims=True)
            m_old = m_scr[g, rs, :]
            m_new = jnp.maximum(m_old, jnp.broadcast_to(blk_max, (rt, LANES)))
            corr = jnp.exp2(m_old - m_new)
            p_sum = jnp.zeros((rt, LANES), F32)
            for cs, lg_c in zip(cols, lg):
                p = jnp.exp2(lg_c - m_new)
                p_scr[g, rs, cs] = p.astype(BF16)
                p_sum = p_sum + p
            l_scr[g, rs, :] = corr * l_scr[g, rs, :] + p_sum
            corr_scr[g, rs, :] = corr
            m_scr[g, rs, :] = m_new

    def attend(j, carry):
        pv(jnp.maximum(j - 1, 0), 1)
        qk(j, 1)
        softmax(0)
        pv(j, 0)
        qk(jnp.minimum(j + 1, nkb - 1), 0)
        softmax(1)
        return carry

    p_scr[1] = jnp.zeros(p_scr.shape[1:], BF16)
    corr_scr[1] = jnp.zeros(corr_scr.shape[1:], F32)
    qk(0, 0)
    lax.fori_loop(0, nkb, attend, 0)
    pv(nkb - 1, 1)
    for g in range(N_KV_HEADS):
        o = acc_scr[g] / jnp.sum(l_scr[g], axis=1, keepdims=True)
        for r in range(HEADS_PER_KV):
            hh = g * HEADS_PER_KV + r
            o_ref[0, :, hh * HEAD_DIM:(hh + 1) * HEAD_DIM] = o[r * tq:(r + 1) * tq].astype(o_ref.dtype)


def _dsa_prompt(q, qi, kw, ki4, k_bf, v_bf, *, kb):
    s_total, t_q, qw = q.shape
    tq = LANES
    lp = ki4.shape[1]
    assert lp == t_q and t_q % tq == 0 and lp % kb == 0 and kb % (2 * LANES) == 0
    topk = min(TOPK_MAX, lp // 4)
    kernel = functools.partial(_dsa_prompt_kernel, kb=kb, topk=topk)
    qmap = lambda b, j: (b, j, 0)
    kmap = lambda b, j: (b, 0, 0)
    rows_g = HEADS_PER_KV * tq
    stat = pltpu.VMEM((N_KV_HEADS, rows_g, LANES), F32)
    return pl.pallas_call(
        kernel,
        grid=(s_total, t_q // tq),
        in_specs=[
            pl.BlockSpec((1, tq, qw), qmap),
            pl.BlockSpec((1, tq, IDX_COLS), qmap),
            pl.BlockSpec((1, tq, LANES), qmap),
            pl.BlockSpec((1, lp, 2 * LANES), kmap, pipeline_mode=pl.Buffered(1)),
            pl.BlockSpec((1, lp, k_bf.shape[2]), kmap, pipeline_mode=pl.Buffered(1)),
            pl.BlockSpec((1, lp, v_bf.shape[2]), kmap, pipeline_mode=pl.Buffered(1)),
        ],
        out_specs=pl.BlockSpec((1, tq, qw), qmap),
        out_shape=jax.ShapeDtypeStruct((s_total, t_q, qw), BF16),
        scratch_shapes=[
            pltpu.VMEM((lp // kb, kb, tq), F32),
            pltpu.VMEM((N_IDX_HEADS * tq, 2 * LANES), BF16),
            pltpu.VMEM((N_IDX_HEADS, tq, LANES), F32),
            pltpu.VMEM((N_KV_HEADS, rows_g, HEAD_DIM + tq), BF16),
            pltpu.VMEM((N_KV_HEADS, kb, HEAD_DIM + tq), BF16),
            pltpu.VMEM((kb, tq), BF16),
            pltpu.VMEM((N_KV_HEADS, rows_g, HEAD_DIM), F32),
            stat, stat,
            pltpu.VMEM((N_KV_HEADS, rows_g, kb), F32),
            pltpu.VMEM((N_KV_HEADS, rows_g, kb), BF16),
            stat,
            pltpu.VMEM((2, N_IDX_HEADS * tq, kb // 2), F32),
            pltpu.VMEM((1, tq), F32),
        ],
        compiler_params=_cparams(2),
        name="dsa_prompt",
    )(q, qi, kw, ki4, k_bf, v_bf)


def _mix_kernel(x_ref, gt_ref, gpost_ref, att_ref, u_ref, vn_ref, gates_ref, ws_ref, bs_ref,
                wpa_ref, wpb_ref, wout_ref, o_ref, sgu_scr):
    nb, t, d = x_ref.shape
    rows = nb * t
    gd = d // N_SGU_GROUPS
    dot = functools.partial(jnp.dot, preferred_element_type=F32)
    r = lax.broadcasted_iota(jnp.int32, (MLP_CHUNK, MLP_CHUNK), 0)
    c = lax.broadcasted_iota(jnp.int32, (MLP_CHUNK, MLP_CHUNK), 1)
    for g in range(N_SGU_GROUPS):
        wg = jnp.where(r >= c, ws_ref[g], 0.0).astype(BF16)
        gs = slice(g * gd, (g + 1) * gd)
        for ch in range(rows // MLP_CHUNK):
            rs = slice(ch * MLP_CHUNK, (ch + 1) * MLP_CHUNK)
            mixed = dot(wg, vn_ref[rs, gs].astype(BF16)) + bs_ref[:, gs]
            sgu_scr[rs, gs] = (u_ref[rs, gs].astype(F32) * mixed).astype(BF16)
    a = dot(att_ref[...], wpa_ref[...])
    b = dot(sgu_scr[...], wpb_ref[...])
    merged = gates_ref[:, 0:d].astype(F32) * a + gates_ref[:, d:2 * d].astype(F32) * b
    h = dot(merged.astype(BF16), wout_ref[...])
    hn = _rms(h, gpost_ref[...])
    o_ref[...] = x_ref[...] + gt_ref[...] * hn.reshape(nb, t, d)


def _row_blocking(s_total, t_full, nb):
    if nb == 1:
        t = min(t_full, 512)
        grid = (s_total, t_full // t)
        xmap = lambda b, j: (b, j, 0)
        mmap = lambda b, j: (b, 0, 0)
        rmap = lambda b, j: (b * (t_full // t) + j, 0)
    else:
        t = t_full
        grid = (s_total // nb,)
        xmap = lambda b: (b, 0, 0)
        mmap = lambda b: (b, 0, 0)
        rmap = lambda b: (b, 0)
    return t, grid, xmap, mmap, rmap


def _mix(x, gt, gpost, att, u, vn, gates, ws, bs, wpa, wpb, wout, nb):
    s_total, t_full, d = x.shape
    t, grid, xmap, mmap, rmap = _row_blocking(s_total, t_full, nb)
    rows = nb * t
    return pl.pallas_call(
        _mix_kernel,
        grid=grid,
        in_specs=[
            pl.BlockSpec((nb, t, d), xmap),
            pl.BlockSpec((nb, 1, d), mmap),
            _const_spec((1, 1, d)),
            pl.BlockSpec((rows, d), rmap),
            pl.BlockSpec((rows, d), rmap),
            pl.BlockSpec((rows, d), rmap),
            pl.BlockSpec((rows, 2 * d), rmap),
            _const_spec(ws.shape),
            _const_spec(bs.shape),
            _const_spec(wpa.shape),
            _const_spec(wpb.shape),
            _const_spec(wout.shape),
        ],
        out_specs=pl.BlockSpec((nb, t, d), xmap),
        out_shape=jax.ShapeDtypeStruct(x.shape, F32),
        scratch_shapes=[pltpu.VMEM((rows, d), BF16)],
        compiler_params=_cparams(len(grid)),
        name="mix",
    )(x, gt, gpost.reshape(1, 1, d), att, u, vn, gates, ws, bs, wpa, wpb, wout)


def _ffn_kernel(x_ref, sh_ref, sc_ref, gt_ref, gpre_ref, gpost_ref, win_ref, wout_ref, o_ref,
                *, n_split):
    nb, t, d = x_ref.shape
    rows = nb * t
    hidden = wout_ref.shape[0]
    hc = hidden // n_split
    dot = functools.partial(jnp.dot, preferred_element_type=F32)
    x = x_ref[...]
    xn = (_rms(x, gpre_ref[...]) * (1.0 + sc_ref[...]) + sh_ref[...]).reshape(rows, d).astype(BF16)
    acc = jnp.zeros((rows, d), F32)
    for i in range(n_split):
        gg = dot(xn, win_ref[:, i * hc:(i + 1) * hc])
        up = dot(xn, win_ref[:, hidden + i * hc:hidden + (i + 1) * hc])
        hmid = (gg * jax.nn.sigmoid(gg) * up).astype(BF16)
        acc = acc + dot(hmid, wout_ref[i * hc:(i + 1) * hc, :])
    hn = _rms(acc, gpost_ref[...])
    o_ref[...] = x + gt_ref[...] * hn.reshape(nb, t, d)


def _ffn(x, sh, sc, gt, gpre, gpost, win, wout, nb):
    s_total, t_full, d = x.shape
    t, grid, xmap, mmap, _ = _row_blocking(s_total, t_full, nb)
    hidden = wout.shape[0]
    n_split = 2
    assert hidden % (n_split * LANES) == 0
    return pl.pallas_call(
        functools.partial(_ffn_kernel, n_split=n_split),
        grid=grid,
        in_specs=[
            pl.BlockSpec((nb, t, d), xmap),
            pl.BlockSpec((nb, 1, d), mmap),
            pl.BlockSpec((nb, 1, d), mmap),
            pl.BlockSpec((nb, 1, d), mmap),
            _const_spec((1, 1, d)),
            _const_spec((1, 1, d)),
            _const_spec(win.shape),
            _const_spec(wout.shape),
        ],
        out_specs=pl.BlockSpec((nb, t, d), xmap),
        out_shape=jax.ShapeDtypeStruct(x.shape, F32),
        compiler_params=_cparams(len(grid)),
        name="ffn",
    )(x, sh, sc, gt, gpre.reshape(1, 1, d), gpost.reshape(1, 1, d), win, wout)


def _prep_weights(w_in, w_proj_a, w_proj_b, w_out, w_ffn_in, w_ffn_out, d):
    qw, kvw = N_HEADS * HEAD_DIM, N_KV_HEADS * HEAD_DIM
    o_q, o_k, o_v = 0, qw, qw + kvw
    o_qi = qw + 2 * kvw
    o_ki = o_qi + IDX_COLS
    o_wi = o_ki + IDX_DIM
    o_u = o_wi + N_IDX_HEADS
    wmain = jnp.concatenate([w_in[:, o_q:o_qi], w_in[:, o_u:]], axis=1).astype(BF16)
    pad = jnp.zeros((d, IDX_PAD - IDX_COLS - IDX_DIM - N_IDX_HEADS), F32)
    widx = jnp.concatenate([w_in[:, o_qi:o_u], pad], axis=1)
    wih, wil = _split_bf16(widx)
    return dict(
        wmain=wmain, wih=wih, wil=wil,
        wpa=w_proj_a.astype(BF16), wpb=w_proj_b.astype(BF16), wout=w_out.astype(BF16),
        wfi=w_ffn_in.astype(BF16), wfo=w_ffn_out.astype(BF16),
    )


def _spatial_tables(w_spatial, b_spatial, seq_len, d):
    cl = min(seq_len, MLP_CHUNK)
    reps = MLP_CHUNK // cl
    gd = d // N_SGU_GROUPS
    w = w_spatial[:, :cl, :cl]
    if reps > 1:
        eye = jnp.eye(reps, dtype=w.dtype)
        w = jnp.einsum("ab,gts->gatbs", eye, w).reshape(N_SGU_GROUPS, MLP_CHUNK, MLP_CHUNK)
    b = jnp.tile(b_spatial[:, :cl].T, (reps, 1))
    b = jnp.repeat(b, gd, axis=1)
    return w, b


def _stream(x, mod, pos, wts, g_pre_mix, g_post_mix, g_pre_ffn, g_post_ffn, b_gate, g_sgu,
            w_spatial, b_spatial, past, nb):
    s_total, t_full, d = x.shape
    sh1, sc1, gt1, sh2, sc2, gt2 = [m[:, None, :] for m in jnp.split(mod, 6, axis=-1)]

    idx_scale = np.float32((IDX_DIM * N_IDX_HEADS) ** -0.5)
    kw_scale = jnp.ones((LANES,), F32).at[WI_LANE:WI_LANE + N_IDX_HEADS].set(idx_scale)
    if nb == 1:
        tpos = pos
        tab_map = lambda j: (j, 0)
    else:
        tpos = jnp.tile(pos, nb)
        tab_map = lambda j: (0, 0)
    tabs = (_rope_tables(tpos, HEAD_DIM), _rope_tables(tpos, IDX_DIM), _kw_tables(tpos, kw_scale))

    vn_dtype = F32 if past is not None else BF16
    (q, k, v, k_bf, v_bf, qi, kw, ki4, u, vn, gates) = _proj(
        x, sh1, sc1, g_pre_mix, wts["wmain"], wts["wih"], wts["wil"], tabs, tab_map,
        b_gate, g_sgu, nb, vn_dtype)

    shp = lambda a: a.reshape(s_total, t_full, a.shape[-1])
    if past is None:
        n_keys = t_full
        att = _dsa_prompt(shp(q), shp(qi), shp(kw), shp(ki4), shp(k_bf), shp(v_bf),
                          kb=min(DSA_KEY_BLOCK, t_full))
    else:
        pk, pv, pki = past
        rows_view = lambda a: a.reshape(s_total, -1, HEAD_DIM)
        att = _dsa_sample(shp(q), shp(qi), shp(kw), pki, shp(ki4), rows_view(pk), rows_view(k),
                          rows_view(pv), rows_view(v))
    att = att.reshape(s_total * t_full, -1)

    ws, bs = _spatial_tables(w_spatial, b_spatial, t_full, d)
    x1 = _mix(x, gt1, g_post_mix, att, u, vn, gates, ws, bs,
              wts["wpa"], wts["wpb"], wts["wout"], nb)
    x2 = _ffn(x1, sh2, sc2, gt2, g_pre_ffn, g_post_ffn, wts["wfi"], wts["wfo"], nb)
    heads = lambda a: a.reshape(s_total, t_full, N_KV_HEADS, HEAD_DIM)
    return x2, heads(k), heads(v), shp(kw)[..., :IDX_DIM], shp(vn)


def _kw_tables(pos, scale_lanes):
    tab = _rope_tables(pos, IDX_DIM)
    lane = jnp.arange(LANES)
    c = jnp.where(lane[None, :] < IDX_DIM, tab[0], 1.0) * scale_lanes[None, :]
    s = jnp.where(lane[None, :] < IDX_DIM, tab[1], 0.0)
    return jnp.stack([c, s])


def kernel(x_prompt, x_sample, cache_k, cache_v, cache_idx_k, c_prompt, c_sample, w_ada, b_ada,
           g_pre_mix, g_post_mix, g_pre_ffn, g_post_ffn, w_in, b_gate, g_sgu, w_spatial,
           b_spatial, w_proj_a, w_proj_b, w_out, w_ffn_in, w_ffn_out):
    depth = w_ada.shape[0]
    bp, t_p, d = x_prompt.shape
    bs_, t_s, _ = x_sample.shape
    past_len = cache_k.shape[2]
    pos_p = jnp.arange(t_p, dtype=jnp.int32)
    pos_s = past_len + jnp.arange(t_s, dtype=jnp.int32)
    nb_s = max(1, min(bs_, 256 // t_s))

    xp, xs = x_prompt, x_sample
    outs = [[] for _ in range(7)]
    for l in range(depth):
        wts = _prep_weights(w_in[l], w_proj_a[l], w_proj_b[l], w_out[l], w_ffn_in[l],
                            w_ffn_out[l], d)
        c_all = jnp.concatenate([c_prompt, c_sample], axis=0)
        pad_rows = (-c_all.shape[0]) % 8
        c_all = jnp.pad(c_all, ((0, pad_rows), (0, 0)))
        mod = _ada(c_all, w_ada[l], b_ada[l])
        common = (wts, g_pre_mix[l], g_post_mix[l], g_pre_ffn[l], g_post_ffn[l], b_gate[l],
                  g_sgu[l], w_spatial[l], b_spatial[l])
        xp, kp, vp, kip, _ = _stream(xp, mod[:bp], pos_p, *common, None, 1)
        past = (cache_k[l], cache_v[l], cache_idx_k[l])
        xs, ks, vs, kis, vns = _stream(xs, mod[bp:bp + bs_], pos_s, *common, past, nb_s)
        for lst, val in zip(outs, (kp, vp, kip, ks, vs, kis, vns)):
            lst.append(val)
    return (xp, xs) + tuple(jnp.stack(o) for o in outs)
```

```python
import functools

import jax
import jax.numpy as jnp
import numpy as np
from jax import lax
from jax.experimental import pallas as pl
from jax.experimental.pallas import tpu as pltpu

F32 = jnp.float32
BF16 = jnp.bfloat16

CHUNK = 64
N_HEADS = 8
N_KV_HEADS = 2
HEAD_DIM = 128
N_IDX_HEADS = 8
IDX_DIM = 64
TOPK_MAX = 256
ROPE_THETA = 500000.0
ROT_FRACTION = 4
MLP_CHUNK = 128
N_SGU_GROUPS = 8
NEG = -1e30
BIG = 3e38
EPS = 1e-6

LANES = 128
VMEM_LIMIT_BYTES = 56 * 1024 * 1024

HEADS_PER_KV = N_HEADS // N_KV_HEADS
IDX_COLS = N_IDX_HEADS * IDX_DIM
IDX_PAD = IDX_COLS + LANES
WI_LANE = IDX_DIM
DSA_Q_BLOCK = 128
DSA_KEY_BLOCK = 512
ATT_ROW_TILE = 32
LOG2E = float(np.log2(np.e))
BISECT_PREFIX = 12
PEEL_POP = 4
BISECT_CAP = 40


def _cparams(n_axes):
    return pltpu.CompilerParams(
        dimension_semantics=("arbitrary",) * n_axes,
        vmem_limit_bytes=VMEM_LIMIT_BYTES,
    )


def _const_spec(shape):
    nd = len(shape)
    return pl.BlockSpec(shape, lambda *_: (0,) * nd, pipeline_mode=pl.Buffered(1))


def _split_bf16(x):
    hi = x.astype(BF16)
    lo = (x - hi.astype(F32)).astype(BF16)
    return hi, lo


def _gelu(x):
    c = np.float32(np.sqrt(2.0 / np.pi))
    return 0.5 * x * (1.0 + jnp.tanh(c * (x + 0.044715 * (x * x * x))))


def _rms(x, g):
    return x * lax.rsqrt(jnp.mean(x * x, axis=-1, keepdims=True) + EPS) * g


def _ada_kernel(c_ref, w_ref, b_ref, o_ref):
    c = c_ref[...]
    a = c * jax.nn.sigmoid(c)
    ah, al = _split_bf16(a)
    wh, wl = _split_bf16(w_ref[...])
    dot = functools.partial(jnp.dot, preferred_element_type=F32)
    o_ref[...] = dot(ah, wh) + dot(al, wh) + dot(ah, wl) + b_ref[...]


def _ada(c, w_ada, b_ada):
    rows, d = c.shape
    n = w_ada.shape[1]
    tn = 1024
    return pl.pallas_call(
        _ada_kernel,
        grid=(n // tn,),
        in_specs=[
            pl.BlockSpec((rows, d), lambda j: (0, 0)),
            pl.BlockSpec((d, tn), lambda j: (0, j)),
            pl.BlockSpec((1, tn), lambda j: (0, j)),
        ],
        out_specs=pl.BlockSpec((rows, tn), lambda j: (0, j)),
        out_shape=jax.ShapeDtypeStruct((rows, n), F32),
        compiler_params=_cparams(1),
        name="ada",
    )(c, w_ada, b_ada.reshape(1, n))


def _rope_tables(pos, head_dim):
    rd = head_dim // ROT_FRACTION
    half = rd // 2
    inv = jnp.power(ROPE_THETA, -(jnp.arange(half, dtype=F32) / half))
    ang = pos.astype(F32)[:, None] * inv[None, :]
    cos, sin = jnp.cos(ang), jnp.sin(ang)
    t = pos.shape[0]
    ones = jnp.ones((t, head_dim - rd), F32)
    zeros = jnp.zeros((t, head_dim - rd), F32)
    c = jnp.concatenate([cos, cos, ones], axis=-1)
    s = jnp.concatenate([-sin, sin, zeros], axis=-1)
    reps = LANES // head_dim
    c, s = jnp.tile(c, (1, reps)), jnp.tile(s, (1, reps))
    return jnp.stack([c, s])


def _rope_apply(x, tab_ref, half):
    lane = lax.broadcasted_iota(jnp.int32, x.shape, 1)
    fwd = pltpu.roll(x, LANES - half, 1)
    bwd = pltpu.roll(x, half, 1)
    partner = jnp.where((lane % (2 * half)) < half, fwd, bwd)
    return x * tab_ref[0] + partner * tab_ref[1]


def _proj_kernel(x_ref, sh_ref, sc_ref, gpre_ref, wmain_ref, wih_ref, wil_ref,
                 tq_ref, ti_ref, tk_ref, bgate_ref, gsgu_ref,
                 q_ref, k_ref, v_ref, kbf_ref, vbf_ref, qi_ref, kw_ref, ki4_ref,
                 u_ref, vn_ref, gates_ref):
    nb, t, d = x_ref.shape
    rows = nb * t
    xn = _rms(x_ref[...], gpre_ref[...]) * (1.0 + sc_ref[...]) + sh_ref[...]
    xn = xn.reshape(rows, d)
    xh, xl = _split_bf16(xn)
    dot = functools.partial(jnp.dot, preferred_element_type=F32)
    q_scale = np.float32(HEAD_DIM ** -0.5 * LOG2E)
    qw = N_HEADS * HEAD_DIM
    kvw = N_KV_HEADS * HEAD_DIM
    half_h = HEAD_DIM // ROT_FRACTION // 2
    half_i = IDX_DIM // ROT_FRACTION // 2

    y = dot(xh, wmain_ref[:, 0:qw])
    for h in range(N_HEADS):
        sl = slice(h * HEAD_DIM, (h + 1) * HEAD_DIM)
        q_ref[:, sl] = (_rope_apply(y[:, sl], tq_ref, half_h) * q_scale).astype(BF16)

    y = dot(xh, wmain_ref[:, qw:qw + kvw])
    for h in range(N_KV_HEADS):
        sl = slice(h * HEAD_DIM, (h + 1) * HEAD_DIM)
        kr = _rope_apply(y[:, sl], tq_ref, half_h)
        k_ref[pl.ds(h, rows, stride=N_KV_HEADS), :] = kr
        kbf_ref[:, sl] = kr.astype(BF16)
    y = dot(xh, wmain_ref[:, qw + kvw:qw + 2 * kvw])
    for h in range(N_KV_HEADS):
        v_ref[pl.ds(h, rows, stride=N_KV_HEADS), :] = y[:, h * HEAD_DIM:(h + 1) * HEAD_DIM]
    vbf_ref[...] = y.astype(BF16)

    wih = wih_ref[...]
    yi = dot(xh, wih) + dot(xl, wih) + dot(xh, wil_ref[...])
    for tile in range(IDX_COLS // LANES):
        sl = slice(tile * LANES, (tile + 1) * LANES)
        qi_ref[:, sl] = _rope_apply(yi[:, sl], ti_ref, half_i)
    kw = _rope_apply(yi[:, IDX_COLS:IDX_PAD], tk_ref, half_i)
    kw_ref[...] = kw
    lane = lax.broadcasted_iota(jnp.int32, kw.shape, 1)
    kk = jnp.where(lane < IDX_DIM, kw, pltpu.roll(kw, IDX_DIM, 1))
    kh, kl = _split_bf16(kk)
    ki4_ref[:, 0:LANES] = kh
    ki4_ref[:, LANES:2 * LANES] = kl

    base = qw + 2 * kvw
    u_ref[...] = _gelu(dot(xh, wmain_ref[:, base:base + d])).astype(u_ref.dtype)
    g = _gelu(dot(xh, wmain_ref[:, base + d:base + 2 * d]))
    mu = jnp.mean(g, axis=-1, keepdims=True)
    gc = g - mu
    vn = gc * lax.rsqrt(jnp.mean(gc * gc, axis=-1, keepdims=True) + EPS) * gsgu_ref[...]
    vn_ref[...] = vn.astype(vn_ref.dtype)

    y = dot(xh, wmain_ref[:, base + 2 * d:base + 4 * d]) + bgate_ref[...]
    gates_ref[...] = jax.nn.sigmoid(y).astype(BF16)


def _proj(x, sh, sc, gpre, wmain, wih, wil, tabs, tab_map, bgate, gsgu, nb, vn_dtype):
    s_total, t_full, d = x.shape
    tq, ti, tk = tabs
    if nb == 1:
        t = min(t_full, 512)
        grid = (s_total, t_full // t)
        xmap = lambda b, j: (b, j, 0)
        mmap = lambda b, j: (b, 0, 0)
        rmap = lambda b, j: (b * (t_full // t) + j, 0)
        tmap = lambda b, j: (0,) + tab_map(j)
    else:
        t = t_full
        grid = (s_total // nb,)
        xmap = lambda b: (b, 0, 0)
        mmap = lambda b: (b, 0, 0)
        rmap = lambda b: (b, 0)
        tmap = lambda b: (0,) + tab_map(0)
    rows = nb * t
    n_rows = s_total * t_full
    qw, kvw = N_HEADS * HEAD_DIM, N_KV_HEADS * HEAD_DIM

    def row_spec(width, per_token=1):
        return pl.BlockSpec((rows * per_token, width), rmap)

    def row_shape(width, dtype, per_token=1):
        return jax.ShapeDtypeStruct((n_rows * per_token, width), dtype)

    tab_spec = pl.BlockSpec((2, rows, LANES), tmap)
    outs = [(qw, BF16), (HEAD_DIM, F32, N_KV_HEADS), (HEAD_DIM, F32, N_KV_HEADS), (kvw, BF16),
            (kvw, BF16), (IDX_COLS, F32), (LANES, F32), (2 * LANES, BF16), (d, BF16),
            (d, vn_dtype), (2 * d, BF16)]
    return pl.pallas_call(
        _proj_kernel,
        grid=grid,
        in_specs=[
            pl.BlockSpec((nb, t, d), xmap),
            pl.BlockSpec((nb, 1, d), mmap),
            pl.BlockSpec((nb, 1, d), mmap),
            _const_spec((1, 1, d)),
            _const_spec(wmain.shape),
            _const_spec(wih.shape),
            _const_spec(wil.shape),
            tab_spec, tab_spec, tab_spec,
            _const_spec((1, 2 * d)),
            _const_spec((1, d)),
        ],
        out_specs=[row_spec(o[0], *o[2:]) for o in outs],
        out_shape=[row_shape(*o) for o in outs],
        compiler_params=_cparams(len(grid)),
        name="proj",
    )(x, sh, sc, gpre.reshape(1, 1, d), wmain, wih, wil, tq, ti, tk,
      bgate.reshape(1, 2 * d), gsgu.reshape(1, d))


def _dsa_sample_kernel(q_ref, qi_ref, kw_ref, kic_ref, ki4n_ref, kc_ref, kn_ref, vc_ref, vn_ref,
                       o_ref, scr, qi4_scr, wb_scr, qg_scr, acc_scr, m_scr, l_scr, thr_scr, lo_scr,
                       hi_scr, hit_scr, bias_scr, lg_scr, p_scr, corr_scr, s_scr, ki4_scr, k_scr,
                       v_scr, *, tq, kb, topk, n_keys):
    n_past = kic_ref.shape[1]
    n_adm = jnp.full((tq, 1), n_keys, jnp.int32)
    nkb = 1
    lane128 = lax.broadcasted_iota(jnp.int32, (tq, LANES), 1)
    nt = (((1,), (1,)), ((), ()))
    dot = functools.partial(jnp.dot, preferred_element_type=F32)

    sub = lax.broadcasted_iota(jnp.int32, (IDX_DIM, LANES), 0)
    lane = lax.broadcasted_iota(jnp.int32, (IDX_DIM, LANES), 1)
    dup = jnp.where(sub == lane % IDX_DIM, 1.0, 0.0).astype(BF16)
    kih, kil = _split_bf16(kic_ref[0])
    ki4_scr[0:n_past, 0:LANES] = dot(kih, dup).astype(BF16)
    ki4_scr[0:n_past, LANES:2 * LANES] = dot(kil, dup).astype(BF16)
    ki4_scr[n_past:n_keys, :] = ki4n_ref[0]
    ki4_scr[n_keys:kb, :] = jnp.zeros((kb - n_keys, 2 * LANES), BF16)
    for g in range(N_KV_HEADS):
        for dst, cache, new in ((k_scr, kc_ref, kn_ref), (v_scr, vc_ref, vn_ref)):
            dst[g, 0:n_past, :] = cache[0, pl.ds(g, n_past, stride=N_KV_HEADS), :].astype(BF16)
            dst[g, n_past:n_keys, :] = new[0, pl.ds(g, tq, stride=N_KV_HEADS), :].astype(BF16)
            dst[g, n_keys:kb, :] = jnp.zeros((kb - n_keys, HEAD_DIM), BF16)

    qi = qi_ref[0]
    for h in range(N_IDX_HEADS):
        tile = qi[:, (h // 2) * LANES:(h // 2 + 1) * LANES]
        own = (lane128 < IDX_DIM) if h % 2 == 0 else (lane128 >= IDX_DIM)
        both = jnp.where(own, tile, pltpu.roll(tile, IDX_DIM, 1))
        hi = both.astype(BF16).astype(F32)
        mix = jnp.where(lane128 < IDX_DIM, hi, both - hi).astype(BF16)
        qi4_scr[h * tq:(h + 1) * tq, 0:LANES] = mix
        qi4_scr[h * tq:(h + 1) * tq, LANES:2 * LANES] = mix
        wb_scr[h] = jnp.broadcast_to(kw_ref[0, :, WI_LANE + h:WI_LANE + h + 1], (tq, LANES))
    for g in range(N_KV_HEADS):
        for r in range(HEADS_PER_KV):
            hh = g * HEADS_PER_KV + r
            qg_scr[g, r * tq:(r + 1) * tq, :] = q_ref[0, :, hh * HEAD_DIM:(hh + 1) * HEAD_DIM]

    s_scr[...] = lax.dot_general(qi4_scr[...], ki4_scr[...], nt, preferred_element_type=F32)
    rmin = jnp.full((tq, LANES), BIG, F32)
    rmax = jnp.full((tq, LANES), -BIG, F32)
    for c in range(kb // LANES):
        cs = slice(c * LANES, (c + 1) * LANES)
        acc = jnp.zeros((tq, LANES), F32)
        for h in range(N_IDX_HEADS):
            acc = acc + wb_scr[h] * jnp.maximum(s_scr[h * tq:(h + 1) * tq, cs], 0.0)
        allowed = (c * LANES + lane128) < n_adm
        scr[0, :, cs] = jnp.where(allowed, acc, NEG)
        rmin = jnp.minimum(rmin, jnp.where(allowed, acc, BIG))
        rmax = jnp.maximum(rmax, jnp.where(allowed, acc, -BIG))
    rmin = jnp.min(rmin, axis=1, keepdims=True)
    rmax = jnp.max(rmax, axis=1, keepdims=True)

    def rep(col):
        return jnp.broadcast_to(col, (tq, LANES))

    def over_tiles(tile_fn, init, combine):
        def body(j, acc):
            for c in range(kb // LANES):
                kpos = (j * kb + c * LANES + lane128).astype(F32)
                acc = combine(acc, tile_fn(scr[j, :, c * LANES:(c + 1) * LANES], kpos))
            return acc

        return lax.fori_loop(0, nkb, body, jnp.full((tq, LANES), init, F32))

    def count(pred_fn):
        acc = over_tiles(lambda s, kpos: pred_fn(s, kpos), 0.0, jnp.add)
        return rep(jnp.sum(acc, axis=1, keepdims=True))

    def count_ge(t):
        return count(lambda s, kpos: jnp.where(s >= t, 1.0, 0.0))

    kf = np.float32(topk)
    rmin, rmax = rep(rmin), rep(rmax)
    need = rep(jnp.where(n_adm > topk, 1.0, 0.0))
    lo_scr[...] = rmin
    hi_scr[...] = jnp.where(need > 0.5, rmax, rmin)
    hit_scr[...] = jnp.zeros((tq, LANES), F32)

    def bis_cond(carry):
        it, n_moved = carry
        return jnp.logical_and(it < BISECT_CAP, n_moved > 0.0)

    def bis_body(carry):
        it, _ = carry
        lo, hi = lo_scr[...], hi_scr[...]
        mid = 0.5 * lo + 0.5 * hi
        cnt = count_ge(mid)
        new_lo = jnp.where(cnt >= kf, mid, lo)
        new_hi = jnp.where(cnt > kf, hi, mid)
        lo_scr[...] = new_lo
        hi_scr[...] = new_hi
        hit_scr[...] = jnp.where(cnt == kf, 1.0, hit_scr[...])
        moved = jnp.where(new_lo != lo, 1.0, jnp.where(new_hi != hi, 1.0, 0.0))
        return it + 1, jnp.sum(moved)

    lax.while_loop(bis_cond, bis_body, (jnp.int32(0), jnp.float32(1.0)))
    thr_scr[...] = lo_scr[...]
    tie_f = jnp.where(hit_scr[...] > 0.5, 0.0, need)

    @pl.when(jnp.sum(tie_f) > 0.0)
    def _ties():
        tie = tie_f > 0.5

        def max_below(bound, strict):
            def tile(s, kpos):
                ok = (s < bound) if strict else (s <= bound)
                return jnp.where(ok, s, -BIG)

            return rep(jnp.max(over_tiles(tile, -BIG, jnp.maximum), axis=1, keepdims=True))

        def short(c):
            return jnp.logical_and(tie, c < kf)

        def snap_cond(carry):
            return jnp.sum(jnp.where(short(carry[1]), 1.0, 0.0)) > 0.0

        def snap_body(carry):
            t_old, c_old = carry
            t_new = max_below(t_old, True)
            c_new = count_ge(t_new)
            upd = short(c_old)
            return jnp.where(upd, t_new, t_old), jnp.where(upd, c_new, c_old)

        t0 = max_below(hi_scr[...], False)
        t, _ = lax.while_loop(snap_cond, snap_body, (t0, count_ge(t0)))
        keep = kf - count(lambda s, kpos: jnp.where(s > t, 1.0, 0.0))

        def count_eq_before(cut):
            return count(lambda s, kpos: jnp.where(s == t, jnp.where(kpos < cut, 1.0, 0.0), 0.0))

        def cut_body(_, carry):
            clo, chi = carry
            cmid = jnp.floor(0.5 * (clo + chi))
            ok = count_eq_before(cmid) >= keep
            return jnp.where(ok, clo, cmid), jnp.where(ok, cmid, chi)

        n_total = np.float32(scr.shape[0] * kb)
        steps = int(np.ceil(np.log2(scr.shape[0] * kb))) + 1
        _, cut = lax.fori_loop(0, steps, cut_body, (jnp.zeros((tq, LANES), F32),
                                                    jnp.full((tq, LANES), n_total, F32)))
        cut = jnp.where(tie, cut, n_total)

        def drop(j, carry):
            for c in range(kb // LANES):
                cs = slice(c * LANES, (c + 1) * LANES)
                s = scr[j, :, cs]
                kpos = (j * kb + c * LANES + lane128).astype(F32)
                dropped = jnp.where(s == t, jnp.where(kpos >= cut, 1.0, 0.0), 0.0)
                scr[j, :, cs] = jnp.where(dropped > 0.5, NEG, s)
            return carry

        lax.fori_loop(0, nkb, drop, 0)
        thr_scr[...] = jnp.where(tie, t, thr_scr[...])

    m_scr[...] = jnp.full(m_scr.shape, NEG, F32)
    l_scr[...] = jnp.zeros(l_scr.shape, F32)
    acc_scr[...] = jnp.zeros(acc_scr.shape, F32)
    thr = thr_scr[...]

    rows_g = HEADS_PER_KV * tq
    rt = min(tq, ATT_ROW_TILE)
    cols = [slice(c * LANES, (c + 1) * LANES) for c in range(kb // LANES)]

    def qk(g):
        lg_scr[g] = lax.dot_general(qg_scr[g], k_scr[g], nt, preferred_element_type=F32)

    def pv(g):
        acc_scr[g] = acc_scr[g] * corr_scr[g] + dot(p_scr[g], v_scr[g])

    def softmax(g):
        for i in range(rows_g // rt):
            rs = slice(i * rt, (i + 1) * rt)
            qs = slice((i * rt) % tq, (i * rt) % tq + rt)
            lg = [lg_scr[g, rs, cs] + bias_scr[qs, cs] for cs in cols]
            blk_max = jnp.max(functools.reduce(jnp.maximum, lg), axis=1, keepdims=True)
            m_old = m_scr[g, rs, :]
            m_new = jnp.maximum(m_old, jnp.broadcast_to(blk_max, (rt, LANES)))
            corr = jnp.exp2(m_old - m_new)
            p_sum = jnp.zeros((rt, LANES), F32)
            for cs, lg_c in zip(cols, lg):
                p = jnp.exp2(lg_c - m_new)
                p_scr[g, rs, cs] = p.astype(BF16)
                p_sum = p_sum + p
            l_scr[g, rs, :] = corr * l_scr[g, rs, :] + p_sum
            corr_scr[g, rs, :] = corr
            m_scr[g, rs, :] = m_new

    def select_bias(j):
        for cs in cols:
            bias_scr[:, cs] = jnp.where(scr[j, :, cs] >= thr, 0.0, NEG)

    qk(0)
    qk(1)
    select_bias(0)
    softmax(0)
    pv(0)
    softmax(1)
    pv(1)
    for g in range(N_KV_HEADS):
        o = acc_scr[g] / jnp.sum(l_scr[g], axis=1, keepdims=True)
        for r in range(HEADS_PER_KV):
            hh = g * HEADS_PER_KV + r
            o_ref[0, :, hh * HEAD_DIM:(hh + 1) * HEAD_DIM] = o[r * tq:(r + 1) * tq].astype(o_ref.dtype)


def _dsa_sample(q, qi, kw, kic, ki4n, kc, kn, vc, vn):
    s_total, tq, qw = q.shape
    n_past = kic.shape[1]
    n_keys = n_past + tq
    kb = LANES * ((n_keys + LANES - 1) // LANES)
    topk = min(TOPK_MAX, n_keys // 4)
    kernel = functools.partial(_dsa_sample_kernel, tq=tq, kb=kb, topk=topk, n_keys=n_keys)
    bmap = lambda b: (b, 0, 0)
    rows_g = HEADS_PER_KV * tq
    return pl.pallas_call(
        kernel,
        grid=(s_total,),
        in_specs=[
            pl.BlockSpec((1, tq, qw), bmap),
            pl.BlockSpec((1, tq, IDX_COLS), bmap),
            pl.BlockSpec((1, tq, LANES), bmap),
            pl.BlockSpec((1, n_past, IDX_DIM), bmap),
            pl.BlockSpec((1, tq, 2 * LANES), bmap),
            pl.BlockSpec((1, N_KV_HEADS * n_past, HEAD_DIM), bmap),
            pl.BlockSpec((1, N_KV_HEADS * tq, HEAD_DIM), bmap),
            pl.BlockSpec((1, N_KV_HEADS * n_past, HEAD_DIM), bmap),
            pl.BlockSpec((1, N_KV_HEADS * tq, HEAD_DIM), bmap),
        ],
        out_specs=pl.BlockSpec((1, tq, qw), bmap),
        out_shape=jax.ShapeDtypeStruct((s_total, tq, qw), BF16),
        scratch_shapes=[
            pltpu.VMEM((1, tq, kb), F32),
            pltpu.VMEM((N_IDX_HEADS * tq, 2 * LANES), BF16),
            pltpu.VMEM((N_IDX_HEADS, tq, LANES), F32),
            pltpu.VMEM((N_KV_HEADS, rows_g, HEAD_DIM), BF16),
            pltpu.VMEM((N_KV_HEADS, rows_g, HEAD_DIM), F32),
            pltpu.VMEM((N_KV_HEADS, rows_g, LANES), F32),
            pltpu.VMEM((N_KV_HEADS, rows_g, LANES), F32),
            pltpu.VMEM((tq, LANES), F32),
            pltpu.VMEM((tq, LANES), F32),
            pltpu.VMEM((tq, LANES), F32),
            pltpu.VMEM((tq, LANES), F32),
            pltpu.VMEM((tq, kb), F32),
            pltpu.VMEM((N_KV_HEADS, rows_g, kb), F32),
            pltpu.VMEM((N_KV_HEADS, rows_g, kb), BF16),
            pltpu.VMEM((N_KV_HEADS, rows_g, LANES), F32),
            pltpu.VMEM((N_IDX_HEADS * tq, kb), F32),
            pltpu.VMEM((kb, 2 * LANES), BF16),
            pltpu.VMEM((N_KV_HEADS, kb, HEAD_DIM), BF16),
            pltpu.VMEM((N_KV_HEADS, kb, HEAD_DIM), BF16),
        ],
        compiler_params=_cparams(1),
        name="dsa_sample",
    )(q, qi, kw, kic, ki4n, kc, kn, vc, vn)


def _dsa_prompt_kernel(q_ref, qi_ref, kw_ref, ki4_ref, k_ref, v_ref, o_ref,
                       scr, qi4_scr, wb_scr, qa_scr, ka_scr, bias_scr, acc_scr, m_scr, l_scr,
                       lg_scr, p_scr, corr_scr, s_scr, thr_scr, *, kb, topk):
    tq = LANES
    qb = pl.program_id(1)
    lane_row = lax.broadcasted_iota(jnp.int32, (1, tq), 1)
    n_adm = ((qb * tq + lane_row) // CHUNK + 1) * CHUNK
    nkb = ((qb + 1) * tq + kb - 1) // kb
    lane128 = lax.broadcasted_iota(jnp.int32, (tq, LANES), 1)
    sub128 = lax.broadcasted_iota(jnp.int32, (LANES, tq), 0)
    nt = (((1,), (1,)), ((), ()))
    rows_g = HEADS_PER_KV * tq

    qi = qi_ref[0]
    for h in range(N_IDX_HEADS):
        tile = qi[:, (h // 2) * LANES:(h // 2 + 1) * LANES]
        own = (lane128 < IDX_DIM) if h % 2 == 0 else (lane128 >= IDX_DIM)
        both = jnp.where(own, tile, pltpu.roll(tile, IDX_DIM, 1))
        hi = both.astype(BF16).astype(F32)
        mix = jnp.where(lane128 < IDX_DIM, hi, both - hi).astype(BF16)
        qi4_scr[h * tq:(h + 1) * tq, 0:LANES] = mix
        qi4_scr[h * tq:(h + 1) * tq, LANES:2 * LANES] = mix
        wb_scr[h] = jnp.broadcast_to(kw_ref[0, :, WI_LANE + h:WI_LANE + h + 1], (tq, LANES))
    eye = jnp.where(sub128 == lane128, 1.0, 0.0).astype(BF16)
    for g in range(N_KV_HEADS):
        for r in range(HEADS_PER_KV):
            hh = g * HEADS_PER_KV + r
            qa_scr[g, r * tq:(r + 1) * tq, 0:HEAD_DIM] = (
                q_ref[0, :, hh * HEAD_DIM:(hh + 1) * HEAD_DIM])
            qa_scr[g, r * tq:(r + 1) * tq, HEAD_DIM:HEAD_DIM + tq] = eye

    kh = kb // 2
    n_half = ki4_ref.shape[1] // kh

    def head_products(hidx, slot):
        kblk = ki4_ref[0, pl.ds(pl.multiple_of(hidx * kh, kh), kh), :]
        s_scr[slot] = lax.dot_general(qi4_scr[...], kblk, nt, preferred_element_type=F32)

    def reduce_heads(j, slot, row0, carry):
        rmin, rmax = carry
        for c in range(kh // LANES):
            cs = slice(c * LANES, (c + 1) * LANES)
            acc = jnp.zeros((tq, LANES), F32)
            for h in range(N_IDX_HEADS):
                acc = acc + wb_scr[h] * jnp.maximum(s_scr[slot, h * tq:(h + 1) * tq, cs], 0.0)
            acc_t = acc.T
            allowed = (j * kb + row0 + c * LANES + sub128) < n_adm
            scr[j, row0 + c * LANES:row0 + (c + 1) * LANES, :] = jnp.where(allowed, acc_t, NEG)
            rmin = jnp.minimum(rmin, jnp.where(allowed, acc_t, BIG))
            rmax = jnp.maximum(rmax, jnp.where(allowed, acc_t, -BIG))
        return rmin, rmax

    def score_step(j, carry):
        head_products(2 * j + 1, 1)
        carry = reduce_heads(j, 0, 0, carry)
        head_products(jnp.minimum(2 * j + 2, n_half - 1), 0)
        return reduce_heads(j, 1, kh, carry)

    head_products(0, 0)
    rmin, rmax = lax.fori_loop(
        0, nkb, score_step,
        (jnp.full((LANES, tq), BIG, F32), jnp.full((LANES, tq), -BIG, F32)))
    rmin = jnp.min(rmin, axis=0, keepdims=True)
    rmax = jnp.max(rmax, axis=0, keepdims=True)

    def over_tiles(tile_fn, init, combine):
        def body(j, acc):
            for c in range(kb // LANES):
                kpos = (j * kb + c * LANES + sub128).astype(F32)
                acc = combine(acc, tile_fn(scr[j, c * LANES:(c + 1) * LANES, :], kpos))
            return acc

        return lax.fori_loop(0, nkb, body, jnp.full((LANES, tq), init, F32))

    def fold_rows(x, combine):
        n = x.shape[0]
        while n > 8:
            n //= 2
            x = combine(x[:n], x[n:])
        return x

    def count(pred_fn):
        part = fold_rows(over_tiles(pred_fn, 0.0, jnp.add), jnp.add)
        return jnp.sum(part, axis=0, keepdims=True)

    def count_ge(t):
        return count(lambda s, kpos: jnp.where(s >= t, 1.0, 0.0))

    def max_below(bound, strict):
        def tile(s, kpos):
            ok = (s < bound) if strict else (s <= bound)
            return jnp.where(ok, s, -BIG)

        part = fold_rows(over_tiles(tile, -BIG, jnp.maximum), jnp.maximum)
        return jnp.max(part, axis=0, keepdims=True)

    kf = np.float32(topk)
    few_f = np.float32(PEEL_POP)
    need = jnp.where(n_adm > topk, 1.0, 0.0)

    def bisect_pass(state):
        lo, hi, clo, chi, hit, thr = state
        mid = 0.5 * lo + 0.5 * hi
        cnt = count_ge(mid)
        ge = cnt >= kf
        exact = cnt == kf
        new = (jnp.where(ge, mid, lo), jnp.where(ge, hi, mid),
               jnp.where(ge, cnt, clo), jnp.where(ge, chi, cnt),
               jnp.where(exact, 1.0, hit),
               jnp.where(hit > 0.5, thr, jnp.where(exact, mid, thr)))
        moved = jnp.where(new[0] != lo, 1.0, jnp.where(new[1] != hi, 1.0, 0.0))
        few = (new[2] - new[3]) <= few_f
        pending = jnp.where(new[4] > 0.5, 0.0, jnp.where(few, 0.0, moved))
        return new, pending

    above_max = rmax + (jnp.abs(rmax) * 1e-6 + 1e-30)
    state = (rmin, above_max, n_adm.astype(F32), jnp.zeros((1, tq), F32), 1.0 - need, rmin)

    state, pending = lax.fori_loop(0, BISECT_PREFIX, lambda _, c: bisect_pass(c[0]),
                                   (state, jnp.ones((1, tq), F32)))

    def bis_cond(carry):
        return jnp.logical_and(carry[0] < BISECT_CAP, carry[1] > 0.0)

    def bis_body(carry):
        state, _ = bisect_pass(carry[2])
        state, pending = bisect_pass(state)
        return carry[0] + 2, jnp.max(pending), state

    _, _, state = lax.while_loop(bis_cond, bis_body,
                                 (jnp.int32(BISECT_PREFIX), jnp.max(pending), state))
    lo, hi, clo, chi, hit, thr = state

    open_f = jnp.where(hit > 0.5, 0.0, 1.0)
    few = jnp.where((clo - chi) <= few_f, open_f, 0.0) > 0.5
    rank = kf - chi
    t = hi
    for step in range(1, PEEL_POP):
        t = jnp.where(rank >= np.float32(step), max_below(t, True), t)
    peeled = jnp.where(few, jnp.where(count_ge(t) == kf, 1.0, 0.0), 0.0)
    thr = jnp.where(peeled > 0.5, t, thr)
    thr_scr[...] = thr
    tie_f = open_f * (1.0 - peeled)

    @pl.when(jnp.max(tie_f) > 0.0)
    def _ties():
        tie = tie_f > 0.5

        def short(c):
            return jnp.where(tie, jnp.where(c < kf, 1.0, 0.0), 0.0)

        def snap_cond(carry):
            return jnp.max(short(carry[1])) > 0.0

        def snap_body(carry):
            t_old, c_old = carry
            t_new = max_below(t_old, True)
            c_new = count_ge(t_new)
            upd = short(c_old) > 0.5
            return jnp.where(upd, t_new, t_old), jnp.where(upd, c_new, c_old)

        t0 = max_below(hi, False)
        t, _ = lax.while_loop(snap_cond, snap_body, (t0, count_ge(t0)))
        keep = kf - count(lambda s, kpos: jnp.where(s > t, 1.0, 0.0))

        def count_eq_before(cut):
            return count(lambda s, kpos: jnp.where(s == t, jnp.where(kpos < cut, 1.0, 0.0), 0.0))

        def cut_body(_, carry):
            clo, chi = carry
            cmid = jnp.floor(0.5 * (clo + chi))
            ok = count_eq_before(cmid) >= keep
            return jnp.where(ok, clo, cmid), jnp.where(ok, cmid, chi)

        n_total = np.float32(scr.shape[0] * kb)
        steps = int(np.ceil(np.log2(scr.shape[0] * kb))) + 1
        _, cut = lax.fori_loop(0, steps, cut_body, (jnp.zeros((1, tq), F32),
                                                    jnp.full((1, tq), n_total, F32)))
        cut = jnp.where(tie, cut, n_total)

        def drop(j, carry):
            for c in range(kb // LANES):
                rs = slice(c * LANES, (c + 1) * LANES)
                s = scr[j, rs, :]
                kpos = (j * kb + c * LANES + sub128).astype(F32)
                dropped = jnp.where(s == t, jnp.where(kpos >= cut, 1.0, 0.0), 0.0)
                scr[j, rs, :] = jnp.where(dropped > 0.5, NEG, s)
            return carry

        lax.fori_loop(0, nkb, drop, 0)
        thr_scr[...] = jnp.where(tie, t, thr)

    m_scr[...] = jnp.full(m_scr.shape, NEG, F32)
    l_scr[...] = jnp.zeros(l_scr.shape, F32)
    acc_scr[...] = jnp.zeros(acc_scr.shape, F32)
    thr = thr_scr[...]
    rt = ATT_ROW_TILE
    cols = [slice(c * LANES, (c + 1) * LANES) for c in range(kb // LANES)]

    def qk(j, g):
        koff = pl.ds(pl.multiple_of(j * kb, kb), kb)
        ka_scr[g, :, 0:HEAD_DIM] = k_ref[0, koff, g * HEAD_DIM:(g + 1) * HEAD_DIM]
        if g == 0:
            for rs in cols:
                mask = jnp.where(scr[j, rs, :] >= thr, 0.0, NEG).astype(BF16)
                ka_scr[0, rs, HEAD_DIM:HEAD_DIM + tq] = mask
                bias_scr[rs, :] = mask
        else:
            ka_scr[g, :, HEAD_DIM:HEAD_DIM + tq] = bias_scr[...]
        lg_scr[g] = lax.dot_general(qa_scr[g], ka_scr[g], nt, preferred_element_type=F32)

    def pv(j, g):
        koff = pl.ds(pl.multiple_of(j * kb, kb), kb)
        vg = v_ref[0, koff, g * HEAD_DIM:(g + 1) * HEAD_DIM]
        acc_scr[g] = acc_scr[g] * corr_scr[g] + jnp.dot(p_scr[g], vg, preferred_element_type=F32)

    def softmax(g):
        for i in range(rows_g // rt):
            rs = slice(i * rt, (i + 1) * rt)
            lg = [lg_scr[g, rs, cs] for cs in cols]
            blk_max = jnp.max(functools.reduce(jnp.maximum, lg), axis=1, keepdims=True)
            m_old = m_scr[g, rs, :]
            m_new = jnp.maximum(m_old, jnp.broadcast_to(blk_max, (rt, LANES)))
            corr = jnp.exp2(m_old - m_new)
            p_sum = jnp.zeros((rt, LANES), F32)
            for cs, lg_c in zip(cols, lg):
                p = jnp.exp2(lg_c - m_new)
                p_scr[g, rs, cs] = p.astype(BF16)
                p_sum = p_sum + p
            l_scr[g, rs, :] = corr * l_scr[g, rs, :] + p_sum
            corr_scr[g, rs, :] = corr
            m_scr[g, rs, :] = m_new

    def attend(j, carry):
        pv(jnp.maximum(j - 1, 0), 1)
        qk(j, 1)
        softmax(0)
        pv(j, 0)
        qk(jnp.minimum(j + 1, nkb - 1), 0)
        softmax(1)
        return carry

    p_scr[1] = jnp.zeros(p_scr.shape[1:], BF16)
    corr_scr[1] = jnp.zeros(corr_scr.shape[1:], F32)
    qk(0, 0)
    lax.fori_loop(0, nkb, attend, 0)
    pv(nkb - 1, 1)
    for g in range(N_KV_HEADS):
        o = acc_scr[g] / jnp.sum(l_scr[g], axis=1, keepdims=True)
        for r in range(HEADS_PER_KV):
            hh = g * HEADS_PER_KV + r
            o_ref[0, :, hh * HEAD_DIM:(hh + 1) * HEAD_DIM] = o[r * tq:(r + 1) * tq].astype(o_ref.dtype)


def _dsa_prompt(q, qi, kw, ki4, k_bf, v_bf, *, kb):
    s_total, t_q, qw = q.shape
    tq = LANES
    lp = ki4.shape[1]
    assert lp == t_q and t_q % tq == 0 and lp % kb == 0 and kb % (2 * LANES) == 0
    topk = min(TOPK_MAX, lp // 4)
    kernel = functools.partial(_dsa_prompt_kernel, kb=kb, topk=topk)
    qmap = lambda b, j: (b, j, 0)
    kmap = lambda b, j: (b, 0, 0)
    rows_g = HEADS_PER_KV * tq
    stat = pltpu.VMEM((N_KV_HEADS, rows_g, LANES), F32)
    return pl.pallas_call(
        kernel,
        grid=(s_total, t_q // tq),
        in_specs=[
            pl.BlockSpec((1, tq, qw), qmap),
            pl.BlockSpec((1, tq, IDX_COLS), qmap),
            pl.BlockSpec((1, tq, LANES), qmap),
            pl.BlockSpec((1, lp, 2 * LANES), kmap, pipeline_mode=pl.Buffered(1)),
            pl.BlockSpec((1, lp, k_bf.shape[2]), kmap, pipeline_mode=pl.Buffered(1)),
            pl.BlockSpec((1, lp, v_bf.shape[2]), kmap, pipeline_mode=pl.Buffered(1)),
        ],
        out_specs=pl.BlockSpec((1, tq, qw), qmap),
        out_shape=jax.ShapeDtypeStruct((s_total, t_q, qw), BF16),
        scratch_shapes=[
            pltpu.VMEM((lp // kb, kb, tq), F32),
            pltpu.VMEM((N_IDX_HEADS * tq, 2 * LANES), BF16),
            pltpu.VMEM((N_IDX_HEADS, tq, LANES), F32),
            pltpu.VMEM((N_KV_HEADS, rows_g, HEAD_DIM + tq), BF16),
            pltpu.VMEM((N_KV_HEADS, kb, HEAD_DIM + tq), BF16),
            pltpu.VMEM((kb, tq), BF16),
            pltpu.VMEM((N_KV_HEADS, rows_g, HEAD_DIM), F32),
            stat, stat,
            pltpu.VMEM((N_KV_HEADS, rows_g, kb), F32),
            pltpu.VMEM((N_KV_HEADS, rows_g, kb), BF16),
            stat,
            pltpu.VMEM((2, N_IDX_HEADS * tq, kb // 2), F32),
            pltpu.VMEM((1, tq), F32),
        ],
        compiler_params=_cparams(2),
        name="dsa_prompt",
    )(q, qi, kw, ki4, k_bf, v_bf)


def _mix_kernel(x_ref, gt_ref, gpost_ref, att_ref, u_ref, vn_ref, gates_ref, ws_ref, bs_ref,
                wpa_ref, wpb_ref, wout_ref, o_ref, sgu_scr):
    nb, t, d = x_ref.shape
    rows = nb * t
    gd = d // N_SGU_GROUPS
    dot = functools.partial(jnp.dot, preferred_element_type=F32)
    r = lax.broadcasted_iota(jnp.int32, (MLP_CHUNK, MLP_CHUNK), 0)
    c = lax.broadcasted_iota(jnp.int32, (MLP_CHUNK, MLP_CHUNK), 1)
    for g in range(N_SGU_GROUPS):
        wg = jnp.where(r >= c, ws_ref[g], 0.0).astype(BF16)
        gs = slice(g * gd, (g + 1) * gd)
        for ch in range(rows // MLP_CHUNK):
            rs = slice(ch * MLP_CHUNK, (ch + 1) * MLP_CHUNK)
            mixed = dot(wg, vn_ref[rs, gs].astype(BF16)) + bs_ref[:, gs]
            sgu_scr[rs, gs] = (u_ref[rs, gs].astype(F32) * mixed).astype(BF16)
    a = dot(att_ref[...], wpa_ref[...])
    b = dot(sgu_scr[...], wpb_ref[...])
    merged = gates_ref[:, 0:d].astype(F32) * a + gates_ref[:, d:2 * d].astype(F32) * b
    h = dot(merged.astype(BF16), wout_ref[...])
    hn = _rms(h, gpost_ref[...])
    o_ref[...] = x_ref[...] + gt_ref[...] * hn.reshape(nb, t, d)


def _row_blocking(s_total, t_full, nb):
    if nb == 1:
        t = min(t_full, 512)
        grid = (s_total, t_full // t)
        xmap = lambda b, j: (b, j, 0)
        mmap = lambda b, j: (b, 0, 0)
        rmap = lambda b, j: (b * (t_full // t) + j, 0)
    else:
        t = t_full
        grid = (s_total // nb,)
        xmap = lambda b: (b, 0, 0)
        mmap = lambda b: (b, 0, 0)
        rmap = lambda b: (b, 0)
    return t, grid, xmap, mmap, rmap


def _mix(x, gt, gpost, att, u, vn, gates, ws, bs, wpa, wpb, wout, nb):
    s_total, t_full, d = x.shape
    t, grid, xmap, mmap, rmap = _row_blocking(s_total, t_full, nb)
    rows = nb * t
    return pl.pallas_call(
        _mix_kernel,
        grid=grid,
        in_specs=[
            pl.BlockSpec((nb, t, d), xmap),
            pl.BlockSpec((nb, 1, d), mmap),
            _const_spec((1, 1, d)),
            pl.BlockSpec((rows, d), rmap),
            pl.BlockSpec((rows, d), rmap),
            pl.BlockSpec((rows, d), rmap),
            pl.BlockSpec((rows, 2 * d), rmap),
            _const_spec(ws.shape),
            _const_spec(bs.shape),
            _const_spec(wpa.shape),
            _const_spec(wpb.shape),
            _const_spec(wout.shape),
        ],
        out_specs=pl.BlockSpec((nb, t, d), xmap),
        out_shape=jax.ShapeDtypeStruct(x.shape, F32),
        scratch_shapes=[pltpu.VMEM((rows, d), BF16)],
        compiler_params=_cparams(len(grid)),
        name="mix",
    )(x, gt, gpost.reshape(1, 1, d), att, u, vn, gates, ws, bs, wpa, wpb, wout)


def _ffn_kernel(x_ref, sh_ref, sc_ref, gt_ref, gpre_ref, gpost_ref, win_ref, wout_ref, o_ref,
                *, n_split):
    nb, t, d = x_ref.shape
    rows = nb * t
    hidden = wout_ref.shape[0]
    hc = hidden // n_split
    dot = functools.partial(jnp.dot, preferred_element_type=F32)
    x = x_ref[...]
    xn = (_rms(x, gpre_ref[...]) * (1.0 + sc_ref[...]) + sh_ref[...]).reshape(rows, d).astype(BF16)
    acc = jnp.zeros((rows, d), F32)
    for i in range(n_split):
        gg = dot(xn, win_ref[:, i * hc:(i + 1) * hc])
        up = dot(xn, win_ref[:, hidden + i * hc:hidden + (i + 1) * hc])
        hmid = (gg * jax.nn.sigmoid(gg) * up).astype(BF16)
        acc = acc + dot(hmid, wout_ref[i * hc:(i + 1) * hc, :])
    hn = _rms(acc, gpost_ref[...])
    o_ref[...] = x + gt_ref[...] * hn.reshape(nb, t, d)


def _ffn(x, sh, sc, gt, gpre, gpost, win, wout, nb):
    s_total, t_full, d = x.shape
    t, grid, xmap, mmap, _ = _row_blocking(s_total, t_full, nb)
    hidden = wout.shape[0]
    n_split = 2
    assert hidden % (n_split * LANES) == 0
    return pl.pallas_call(
        functools.partial(_ffn_kernel, n_split=n_split),
        grid=grid,
        in_specs=[
            pl.BlockSpec((nb, t, d), xmap),
            pl.BlockSpec((nb, 1, d), mmap),
            pl.BlockSpec((nb, 1, d), mmap),
            pl.BlockSpec((nb, 1, d), mmap),
            _const_spec((1, 1, d)),
            _const_spec((1, 1, d)),
            _const_spec(win.shape),
            _const_spec(wout.shape),
        ],
        out_specs=pl.BlockSpec((nb, t, d), xmap),
        out_shape=jax.ShapeDtypeStruct(x.shape, F32),
        compiler_params=_cparams(len(grid)),
        name="ffn",
    )(x, sh, sc, gt, gpre.reshape(1, 1, d), gpost.reshape(1, 1, d), win, wout)


def _prep_weights(w_in, w_proj_a, w_proj_b, w_out, w_ffn_in, w_ffn_out, d):
    qw, kvw = N_HEADS * HEAD_DIM, N_KV_HEADS * HEAD_DIM
    o_q, o_k, o_v = 0, qw, qw + kvw
    o_qi = qw + 2 * kvw
    o_ki = o_qi + IDX_COLS
    o_wi = o_ki + IDX_DIM
    o_u = o_wi + N_IDX_HEADS
    wmain = jnp.concatenate([w_in[:, o_q:o_qi], w_in[:, o_u:]], axis=1).astype(BF16)
    pad = jnp.zeros((d, IDX_PAD - IDX_COLS - IDX_DIM - N_IDX_HEADS), F32)
    widx = jnp.concatenate([w_in[:, o_qi:o_u], pad], axis=1)
    wih, wil = _split_bf16(widx)
    return dict(
        wmain=wmain, wih=wih, wil=wil,
        wpa=w_proj_a.astype(BF16), wpb=w_proj_b.astype(BF16), wout=w_out.astype(BF16),
        wfi=w_ffn_in.astype(BF16), wfo=w_ffn_out.astype(BF16),
    )


def _spatial_tables(w_spatial, b_spatial, seq_len, d):
    cl = min(seq_len, MLP_CHUNK)
    reps = MLP_CHUNK // cl
    gd = d // N_SGU_GROUPS
    w = w_spatial[:, :cl, :cl]
    if reps > 1:
        eye = jnp.eye(reps, dtype=w.dtype)
        w = jnp.einsum("ab,gts->gatbs", eye, w).reshape(N_SGU_GROUPS, MLP_CHUNK, MLP_CHUNK)
    b = jnp.tile(b_spatial[:, :cl].T, (reps, 1))
    b = jnp.repeat(b, gd, axis=1)
    return w, b


def _stream(x, mod, pos, wts, g_pre_mix, g_post_mix, g_pre_ffn, g_post_ffn, b_gate, g_sgu,
            w_spatial, b_spatial, past, nb):
    s_total, t_full, d = x.shape
    sh1, sc1, gt1, sh2, sc2, gt2 = [m[:, None, :] for m in jnp.split(mod, 6, axis=-1)]

    idx_scale = np.float32((IDX_DIM * N_IDX_HEADS) ** -0.5)
    kw_scale = jnp.ones((LANES,), F32).at[WI_LANE:WI_LANE + N_IDX_HEADS].set(idx_scale)
    if nb == 1:
        tpos = pos
        tab_map = lambda j: (j, 0)
    else:
        tpos = jnp.tile(pos, nb)
        tab_map = lambda j: (0, 0)
    tabs = (_rope_tables(tpos, HEAD_DIM), _rope_tables(tpos, IDX_DIM), _kw_tables(tpos, kw_scale))

    vn_dtype = F32 if past is not None else BF16
    (q, k, v, k_bf, v_bf, qi, kw, ki4, u, vn, gates) = _proj(
        x, sh1, sc1, g_pre_mix, wts["wmain"], wts["wih"], wts["wil"], tabs, tab_map,
        b_gate, g_sgu, nb, vn_dtype)

    shp = lambda a: a.reshape(s_total, t_full, a.shape[-1])
    if past is None:
        n_keys = t_full
        att = _dsa_prompt(shp(q), shp(qi), shp(kw), shp(ki4), shp(k_bf), shp(v_bf),
                          kb=min(DSA_KEY_BLOCK, t_full))
    else:
        pk, pv, pki = past
        rows_view = lambda a: a.reshape(s_total, -1, HEAD_DIM)
        att = _dsa_sample(shp(q), shp(qi), shp(kw), pki, shp(ki4), rows_view(pk), rows_view(k),
                          rows_view(pv), rows_view(v))
    att = att.reshape(s_total * t_full, -1)

    ws, bs = _spatial_tables(w_spatial, b_spatial, t_full, d)
    x1 = _mix(x, gt1, g_post_mix, att, u, vn, gates, ws, bs,
              wts["wpa"], wts["wpb"], wts["wout"], nb)
    x2 = _ffn(x1, sh2, sc2, gt2, g_pre_ffn, g_post_ffn, wts["wfi"], wts["wfo"], nb)
    heads = lambda a: a.reshape(s_total, t_full, N_KV_HEADS, HEAD_DIM)
    return x2, heads(k), heads(v), shp(kw)[..., :IDX_DIM], shp(vn)


def _kw_tables(pos, scale_lanes):
    tab = _rope_tables(pos, IDX_DIM)
    lane = jnp.arange(LANES)
    c = jnp.where(lane[None, :] < IDX_DIM, tab[0], 1.0) * scale_lanes[None, :]
    s = jnp.where(lane[None, :] < IDX_DIM, tab[1], 0.0)
    return jnp.stack([c, s])


def kernel(x_prompt, x_sample, cache_k, cache_v, cache_idx_k, c_prompt, c_sample, w_ada, b_ada,
           g_pre_mix, g_post_mix, g_pre_ffn, g_post_ffn, w_in, b_gate, g_sgu, w_spatial,
           b_spatial, w_proj_a, w_proj_b, w_out, w_ffn_in, w_ffn_out):
    depth = w_ada.shape[0]
    bp, t_p, d = x_prompt.shape
    bs_, t_s, _ = x_sample.shape
    past_len = cache_k.shape[2]
    pos_p = jnp.arange(t_p, dtype=jnp.int32)
    pos_s = past_len + jnp.arange(t_s, dtype=jnp.int32)
    nb_s = max(1, min(bs_, 256 // t_s))

    xp, xs = x_prompt, x_sample
    outs = [[] for _ in range(7)]
    for l in range(depth):
        wts = _prep_weights(w_in[l], w_proj_a[l], w_proj_b[l], w_out[l], w_ffn_in[l],
                            w_ffn_out[l], d)
        c_all = jnp.concatenate([c_prompt, c_sample], axis=0)
        pad_rows = (-c_all.shape[0]) % 8
        c_all = jnp.pad(c_all, ((0, pad_rows), (0, 0)))
        mod = _ada(c_all, w_ada[l], b_ada[l])
        common = (wts, g_pre_mix[l], g_post_mix[l], g_pre_ffn[l], g_post_ffn[l], b_gate[l],
                  g_sgu[l], w_spatial[l], b_spatial[l])
        xp, kp, vp, kip, _ = _stream(xp, mod[:bp], pos_p, *common, None, 1)
        past = (cache_k[l], cache_v[l], cache_idx_k[l])
        xs, ks, vs, kis, vns = _stream(xs, mod[bp:bp + bs_], pos_s, *common, past, nb_s)
        for lst, val in zip(outs, (kp, vp, kip, ks, vs, kis, vns)):
            lst.append(val)
    return (xp, xs) + tuple(jnp.stack(o) for o in outs)
```

```python
import functools

import jax
import jax.numpy as jnp
import numpy as np
from jax import lax
from jax.experimental import pallas as pl
from jax.experimental.pallas import tpu as pltpu

F32 = jnp.float32
BF16 = jnp.bfloat16

CHUNK = 64
N_HEADS = 8
N_KV_HEADS = 2
HEAD_DIM = 128
N_IDX_HEADS = 8
IDX_DIM = 64
TOPK_MAX = 256
ROPE_THETA = 500000.0
ROT_FRACTION = 4
MLP_CHUNK = 128
N_SGU_GROUPS = 8
NEG = -1e30
BIG = 3e38
EPS = 1e-6

LANES = 128
VMEM_LIMIT_BYTES = 56 * 1024 * 1024

HEADS_PER_KV = N_HEADS // N_KV_HEADS
IDX_COLS = N_IDX_HEADS * IDX_DIM
IDX_PAD = IDX_COLS + LANES
WI_LANE = IDX_DIM
DSA_Q_BLOCK = 128
DSA_KEY_BLOCK = 512
ATT_ROW_TILE = 32
LOG2E = float(np.log2(np.e))
BISECT_PREFIX = 12
PEEL_POP = 4
BISECT_CAP = 40


def _cparams(n_axes):
    return pltpu.CompilerParams(
        dimension_semantics=("arbitrary",) * n_axes,
        vmem_limit_bytes=VMEM_LIMIT_BYTES,
    )


def _const_spec(shape):
    nd = len(shape)
    return pl.BlockSpec(shape, lambda *_: (0,) * nd, pipeline_mode=pl.Buffered(1))


def _split_bf16(x):
    hi = x.astype(BF16)
    lo = (x - hi.astype(F32)).astype(BF16)
    return hi, lo


def _gelu(x):
    c = np.float32(np.sqrt(2.0 / np.pi))
    return 0.5 * x * (1.0 + jnp.tanh(c * (x + 0.044715 * (x * x * x))))


def _rms(x, g):
    return x * lax.rsqrt(jnp.mean(x * x, axis=-1, keepdims=True) + EPS) * g


def _ada_kernel(c_ref, w_ref, b_ref, o_ref):
    c = c_ref[...]
    a = c * jax.nn.sigmoid(c)
    ah, al = _split_bf16(a)
    wh, wl = _split_bf16(w_ref[...])
    dot = functools.partial(jnp.dot, preferred_element_type=F32)
    o_ref[...] = dot(ah, wh) + dot(al, wh) + dot(ah, wl) + b_ref[...]


def _ada(c, w_ada, b_ada):
    rows, d = c.shape
    n = w_ada.shape[1]
    tn = 1024
    return pl.pallas_call(
        _ada_kernel,
        grid=(n // tn,),
        in_specs=[
            pl.BlockSpec((rows, d), lambda j: (0, 0)),
            pl.BlockSpec((d, tn), lambda j: (0, j)),
            pl.BlockSpec((1, tn), lambda j: (0, j)),
        ],
        out_specs=pl.BlockSpec((rows, tn), lambda j: (0, j)),
        out_shape=jax.ShapeDtypeStruct((rows, n), F32),
        compiler_params=_cparams(1),
        name="ada",
    )(c, w_ada, b_ada.reshape(1, n))


def _rope_tables(pos, head_dim):
    rd = head_dim // ROT_FRACTION
    half = rd // 2
    inv = jnp.power(ROPE_THETA, -(jnp.arange(half, dtype=F32) / half))
    ang = pos.astype(F32)[:, None] * inv[None, :]
    cos, sin = jnp.cos(ang), jnp.sin(ang)
    t = pos.shape[0]
    ones = jnp.ones((t, head_dim - rd), F32)
    zeros = jnp.zeros((t, head_dim - rd), F32)
    c = jnp.concatenate([cos, cos, ones], axis=-1)
    s = jnp.concatenate([-sin, sin, zeros], axis=-1)
    reps = LANES // head_dim
    c, s = jnp.tile(c, (1, reps)), jnp.tile(s, (1, reps))
    return jnp.stack([c, s])


def _rope_apply(x, tab_ref, half):
    lane = lax.broadcasted_iota(jnp.int32, x.shape, 1)
    fwd = pltpu.roll(x, LANES - half, 1)
    bwd = pltpu.roll(x, half, 1)
    partner = jnp.where((lane % (2 * half)) < half, fwd, bwd)
    return x * tab_ref[0] + partner * tab_ref[1]


def _proj_kernel(x_ref, sh_ref, sc_ref, gpre_ref, wmain_ref, wih_ref, wil_ref,
                 tq_ref, ti_ref, tk_ref, bgate_ref, gsgu_ref,
                 q_ref, k_ref, v_ref, kbf_ref, vbf_ref, qi_ref, kw_ref, ki4_ref,
                 u_ref, vn_ref, gates_ref):
    nb, t, d = x_ref.shape
    rows = nb * t
    xn = _rms(x_ref[...], gpre_ref[...]) * (1.0 + sc_ref[...]) + sh_ref[...]
    xn = xn.reshape(rows, d)
    xh, xl = _split_bf16(xn)
    dot = functools.partial(jnp.dot, preferred_element_type=F32)
    q_scale = np.float32(HEAD_DIM ** -0.5 * LOG2E)
    qw = N_HEADS * HEAD_DIM
    kvw = N_KV_HEADS * HEAD_DIM
    half_h = HEAD_DIM // ROT_FRACTION // 2
    half_i = IDX_DIM // ROT_FRACTION // 2

    y = dot(xh, wmain_ref[:, 0:qw])
    for h in range(N_HEADS):
        sl = slice(h * HEAD_DIM, (h + 1) * HEAD_DIM)
        q_ref[:, sl] = (_rope_apply(y[:, sl], tq_ref, half_h) * q_scale).astype(BF16)

    y = dot(xh, wmain_ref[:, qw:qw + kvw])
    for h in range(N_KV_HEADS):
        sl = slice(h * HEAD_DIM, (h + 1) * HEAD_DIM)
        kr = _rope_apply(y[:, sl], tq_ref, half_h)
        k_ref[pl.ds(h, rows, stride=N_KV_HEADS), :] = kr
        kbf_ref[:, sl] = kr.astype(BF16)
    y = dot(xh, wmain_ref[:, qw + kvw:qw + 2 * kvw])
    for h in range(N_KV_HEADS):
        v_ref[pl.ds(h, rows, stride=N_KV_HEADS), :] = y[:, h * HEAD_DIM:(h + 1) * HEAD_DIM]
    vbf_ref[...] = y.astype(BF16)

    wih = wih_ref[...]
    yi = dot(xh, wih) + dot(xl, wih) + dot(xh, wil_ref[...])
    for tile in range(IDX_COLS // LANES):
        sl = slice(tile * LANES, (tile + 1) * LANES)
        qi_ref[:, sl] = _rope_apply(yi[:, sl], ti_ref, half_i)
    kw = _rope_apply(yi[:, IDX_COLS:IDX_PAD], tk_ref, half_i)
    kw_ref[...] = kw
    lane = lax.broadcasted_iota(jnp.int32, kw.shape, 1)
    kk = jnp.where(lane < IDX_DIM, kw, pltpu.roll(kw, IDX_DIM, 1))
    kh, kl = _split_bf16(kk)
    ki4_ref[:, 0:LANES] = kh
    ki4_ref[:, LANES:2 * LANES] = kl

    base = qw + 2 * kvw
    u_ref[...] = _gelu(dot(xh, wmain_ref[:, base:base + d])).astype(u_ref.dtype)
    g = _gelu(dot(xh, wmain_ref[:, base + d:base + 2 * d]))
    mu = jnp.mean(g, axis=-1, keepdims=True)
    gc = g - mu
    vn = gc * lax.rsqrt(jnp.mean(gc * gc, axis=-1, keepdims=True) + EPS) * gsgu_ref[...]
    vn_ref[...] = vn.astype(vn_ref.dtype)

    y = dot(xh, wmain_ref[:, base + 2 * d:base + 4 * d]) + bgate_ref[...]
    gates_ref[...] = jax.nn.sigmoid(y).astype(BF16)


def _proj(x, sh, sc, gpre, wmain, wih, wil, tabs, tab_map, bgate, gsgu, nb, vn_dtype):
    s_total, t_full, d = x.shape
    tq, ti, tk = tabs
    if nb == 1:
        t = min(t_full, 512)
        grid = (s_total, t_full // t)
        xmap = lambda b, j: (b, j, 0)
        mmap = lambda b, j: (b, 0, 0)
        rmap = lambda b, j: (b * (t_full // t) + j, 0)
        tmap = lambda b, j: (0,) + tab_map(j)
    else:
        t = t_full
        grid = (s_total // nb,)
        xmap = lambda b: (b, 0, 0)
        mmap = lambda b: (b, 0, 0)
        rmap = lambda b: (b, 0)
        tmap = lambda b: (0,) + tab_map(0)
    rows = nb * t
    n_rows = s_total * t_full
    qw, kvw = N_HEADS * HEAD_DIM, N_KV_HEADS * HEAD_DIM

    def row_spec(width, per_token=1):
        return pl.BlockSpec((rows * per_token, width), rmap)

    def row_shape(width, dtype, per_token=1):
        return jax.ShapeDtypeStruct((n_rows * per_token, width), dtype)

    tab_spec = pl.BlockSpec((2, rows, LANES), tmap)
    outs = [(qw, BF16), (HEAD_DIM, F32, N_KV_HEADS), (HEAD_DIM, F32, N_KV_HEADS), (kvw, BF16),
            (kvw, BF16), (IDX_COLS, F32), (LANES, F32), (2 * LANES, BF16), (d, BF16),
            (d, vn_dtype), (2 * d, BF16)]
    return pl.pallas_call(
        _proj_kernel,
        grid=grid,
        in_specs=[
            pl.BlockSpec((nb, t, d), xmap),
            pl.BlockSpec((nb, 1, d), mmap),
            pl.BlockSpec((nb, 1, d), mmap),
            _const_spec((1, 1, d)),
            _const_spec(wmain.shape),
            _const_spec(wih.shape),
            _const_spec(wil.shape),
            tab_spec, tab_spec, tab_spec,
            _const_spec((1, 2 * d)),
            _const_spec((1, d)),
        ],
        out_specs=[row_spec(o[0], *o[2:]) for o in outs],
        out_shape=[row_shape(*o) for o in outs],
        compiler_params=_cparams(len(grid)),
        name="proj",
    )(x, sh, sc, gpre.reshape(1, 1, d), wmain, wih, wil, tq, ti, tk,
      bgate.reshape(1, 2 * d), gsgu.reshape(1, d))


def _dsa_sample_kernel(q_ref, qi_ref, kw_ref, kic_ref, ki4n_ref, kc_ref, kn_ref, vc_ref, vn_ref,
                       o_ref, scr, qi4_scr, wb_scr, qg_scr, acc_scr, m_scr, l_scr, thr_scr, lo_scr,
                       hi_scr, hit_scr, bias_scr, lg_scr, p_scr, corr_scr, s_scr, ki4_scr, k_scr,
                       v_scr, *, tq, kb, topk, n_keys):
    n_past = kic_ref.shape[1]
    n_adm = jnp.full((tq, 1), n_keys, jnp.int32)
    nkb = 1
    lane128 = lax.broadcasted_iota(jnp.int32, (tq, LANES), 1)
    nt = (((1,), (1,)), ((), ()))
    dot = functools.partial(jnp.dot, preferred_element_type=F32)

    sub = lax.broadcasted_iota(jnp.int32, (IDX_DIM, LANES), 0)
    lane = lax.broadcasted_iota(jnp.int32, (IDX_DIM, LANES), 1)
    dup = jnp.where(sub == lane % IDX_DIM, 1.0, 0.0).astype(BF16)
    kih, kil = _split_bf16(kic_ref[0])
    ki4_scr[0:n_past, 0:LANES] = dot(kih, dup).astype(BF16)
    ki4_scr[0:n_past, LANES:2 * LANES] = dot(kil, dup).astype(BF16)
    ki4_scr[n_past:n_keys, :] = ki4n_ref[0]
    ki4_scr[n_keys:kb, :] = jnp.zeros((kb - n_keys, 2 * LANES), BF16)
    for g in range(N_KV_HEADS):
        for dst, cache, new in ((k_scr, kc_ref, kn_ref), (v_scr, vc_ref, vn_ref)):
            dst[g, 0:n_past, :] = cache[0, pl.ds(g, n_past, stride=N_KV_HEADS), :].astype(BF16)
            dst[g, n_past:n_keys, :] = new[0, pl.ds(g, tq, stride=N_KV_HEADS), :].astype(BF16)
            dst[g, n_keys:kb, :] = jnp.zeros((kb - n_keys, HEAD_DIM), BF16)

    qi = qi_ref[0]
    for h in range(N_IDX_HEADS):
        tile = qi[:, (h // 2) * LANES:(h // 2 + 1) * LANES]
        own = (lane128 < IDX_DIM) if h % 2 == 0 else (lane128 >= IDX_DIM)
        both = jnp.where(own, tile, pltpu.roll(tile, IDX_DIM, 1))
        hi = both.astype(BF16).astype(F32)
        mix = jnp.where(lane128 < IDX_DIM, hi, both - hi).astype(BF16)
        qi4_scr[h * tq:(h + 1) * tq, 0:LANES] = mix
        qi4_scr[h * tq:(h + 1) * tq, LANES:2 * LANES] = mix
        wb_scr[h] = jnp.broadcast_to(kw_ref[0, :, WI_LANE + h:WI_LANE + h + 1], (tq, LANES))
    for g in range(N_KV_HEADS):
        for r in range(HEADS_PER_KV):
            hh = g * HEADS_PER_KV + r
            qg_scr[g, r * tq:(r + 1) * tq, :] = q_ref[0, :, hh * HEAD_DIM:(hh + 1) * HEAD_DIM]

    s_scr[...] = lax.dot_general(qi4_scr[...], ki4_scr[...], nt, preferred_element_type=F32)
    rmin = jnp.full((tq, LANES), BIG, F32)
    rmax = jnp.full((tq, LANES), -BIG, F32)
    for c in range(kb // LANES):
        cs = slice(c * LANES, (c + 1) * LANES)
        acc = jnp.zeros((tq, LANES), F32)
        for h in range(N_IDX_HEADS):
            acc = acc + wb_scr[h] * jnp.maximum(s_scr[h * tq:(h + 1) * tq, cs], 0.0)
        allowed = (c * LANES + lane128) < n_adm
        scr[0, :, cs] = jnp.where(allowed, acc, NEG)
        rmin = jnp.minimum(rmin, jnp.where(allowed, acc, BIG))
        rmax = jnp.maximum(rmax, jnp.where(allowed, acc, -BIG))
    rmin = jnp.min(rmin, axis=1, keepdims=True)
    rmax = jnp.max(rmax, axis=1, keepdims=True)

    def rep(col):
        return jnp.broadcast_to(col, (tq, LANES))

    def over_tiles(tile_fn, init, combine):
        def body(j, acc):
            for c in range(kb // LANES):
                kpos = (j * kb + c * LANES + lane128).astype(F32)
                acc = combine(acc, tile_fn(scr[j, :, c * LANES:(c + 1) * LANES], kpos))
            return acc

        return lax.fori_loop(0, nkb, body, jnp.full((tq, LANES), init, F32))

    def count(pred_fn):
        acc = over_tiles(lambda s, kpos: pred_fn(s, kpos), 0.0, jnp.add)
        return rep(jnp.sum(acc, axis=1, keepdims=True))

    def count_ge(t):
        return count(lambda s, kpos: jnp.where(s >= t, 1.0, 0.0))

    kf = np.float32(topk)
    rmin, rmax = rep(rmin), rep(rmax)
    need = rep(jnp.where(n_adm > topk, 1.0, 0.0))
    lo_scr[...] = rmin
    hi_scr[...] = jnp.where(need > 0.5, rmax, rmin)
    hit_scr[...] = jnp.zeros((tq, LANES), F32)

    def bis_cond(carry):
        it, n_moved = carry
        return jnp.logical_and(it < BISECT_CAP, n_moved > 0.0)

    def bis_body(carry):
        it, _ = carry
        lo, hi = lo_scr[...], hi_scr[...]
        mid = 0.5 * lo + 0.5 * hi
        cnt = count_ge(mid)
        new_lo = jnp.where(cnt >= kf, mid, lo)
        new_hi = jnp.where(cnt > kf, hi, mid)
        lo_scr[...] = new_lo
        hi_scr[...] = new_hi
        hit_scr[...] = jnp.where(cnt == kf, 1.0, hit_scr[...])
        moved = jnp.where(new_lo != lo, 1.0, jnp.where(new_hi != hi, 1.0, 0.0))
        return it + 1, jnp.sum(moved)

    lax.while_loop(bis_cond, bis_body, (jnp.int32(0), jnp.float32(1.0)))
    thr_scr[...] = lo_scr[...]
    tie_f = jnp.where(hit_scr[...] > 0.5, 0.0, need)

    @pl.when(jnp.sum(tie_f) > 0.0)
    def _ties():
        tie = tie_f > 0.5

        def max_below(bound, strict):
            def tile(s, kpos):
                ok = (s < bound) if strict else (s <= bound)
                return jnp.where(ok, s, -BIG)

            return rep(jnp.max(over_tiles(tile, -BIG, jnp.maximum), axis=1, keepdims=True))

        def short(c):
            return jnp.logical_and(tie, c < kf)

        def snap_cond(carry):
            return jnp.sum(jnp.where(short(carry[1]), 1.0, 0.0)) > 0.0

        def snap_body(carry):
            t_old, c_old = carry
            t_new = max_below(t_old, True)
            c_new = count_ge(t_new)
            upd = short(c_old)
            return jnp.where(upd, t_new, t_old), jnp.where(upd, c_new, c_old)

        t0 = max_below(hi_scr[...], False)
        t, _ = lax.while_loop(snap_cond, snap_body, (t0, count_ge(t0)))
        keep = kf - count(lambda s, kpos: jnp.where(s > t, 1.0, 0.0))

        def count_eq_before(cut):
            return count(lambda s, kpos: jnp.where(s == t, jnp.where(kpos < cut, 1.0, 0.0), 0.0))

        def cut_body(_, carry):
            clo, chi = carry
            cmid = jnp.floor(0.5 * (clo + chi))
            ok = count_eq_before(cmid) >= keep
            return jnp.where(ok, clo, cmid), jnp.where(ok, cmid, chi)

        n_total = np.float32(scr.shape[0] * kb)
        steps = int(np.ceil(np.log2(scr.shape[0] * kb))) + 1
        _, cut = lax.fori_loop(0, steps, cut_body, (jnp.zeros((tq, LANES), F32),
                                                    jnp.full((tq, LANES), n_total, F32)))
        cut = jnp.where(tie, cut, n_total)

        def drop(j, carry):
            for c in range(kb // LANES):
                cs = slice(c * LANES, (c + 1) * LANES)
                s = scr[j, :, cs]
                kpos = (j * kb + c * LANES + lane128).astype(F32)
                dropped = jnp.where(s == t, jnp.where(kpos >= cut, 1.0, 0.0), 0.0)
                scr[j, :, cs] = jnp.where(dropped > 0.5, NEG, s)
            return carry

        lax.fori_loop(0, nkb, drop, 0)
        thr_scr[...] = jnp.where(tie, t, thr_scr[...])

    m_scr[...] = jnp.full(m_scr.shape, NEG, F32)
    l_scr[...] = jnp.zeros(l_scr.shape, F32)
    acc_scr[...] = jnp.zeros(acc_scr.shape, F32)
    thr = thr_scr[...]

    rows_g = HEADS_PER_KV * tq
    rt = min(tq, ATT_ROW_TILE)
    cols = [slice(c * LANES, (c + 1) * LANES) for c in range(kb // LANES)]

    def qk(g):
        lg_scr[g] = lax.dot_general(qg_scr[g], k_scr[g], nt, preferred_element_type=F32)

    def pv(g):
        acc_scr[g] = acc_scr[g] * corr_scr[g] + dot(p_scr[g], v_scr[g])

    def softmax(g):
        for i in range(rows_g // rt):
            rs = slice(i * rt, (i + 1) * rt)
            qs = slice((i * rt) % tq, (i * rt) % tq + rt)
            lg = [lg_scr[g, rs, cs] + bias_scr[qs, cs] for cs in cols]
            blk_max = jnp.max(functools.reduce(jnp.maximum, lg), axis=1, keepdims=True)
            m_old = m_scr[g, rs, :]
            m_new = jnp.maximum(m_old, jnp.broadcast_to(blk_max, (rt, LANES)))
            corr = jnp.exp2(m_old - m_new)
            p_sum = jnp.zeros((rt, LANES), F32)
            for cs, lg_c in zip(cols, lg):
                p = jnp.exp2(lg_c - m_new)
                p_scr[g, rs, cs] = p.astype(BF16)
                p_sum = p_sum + p
            l_scr[g, rs, :] = corr * l_scr[g, rs, :] + p_sum
            corr_scr[g, rs, :] = corr
            m_scr[g, rs, :] = m_new

    def select_bias(j):
        for cs in cols:
            bias_scr[:, cs] = jnp.where(scr[j, :, cs] >= thr, 0.0, NEG)

    qk(0)
    qk(1)
    select_bias(0)
    softmax(0)
    pv(0)
    softmax(1)
    pv(1)
    for g in range(N_KV_HEADS):
        o = acc_scr[g] / jnp.sum(l_scr[g], axis=1, keepdims=True)
        for r in range(HEADS_PER_KV):
            hh = g * HEADS_PER_KV + r
            o_ref[0, :, hh * HEAD_DIM:(hh + 1) * HEAD_DIM] = o[r * tq:(r + 1) * tq].astype(o_ref.dtype)


def _dsa_sample(q, qi, kw, kic, ki4n, kc, kn, vc, vn):
    s_total, tq, qw = q.shape
    n_past = kic.shape[1]
    n_keys = n_past + tq
    kb = LANES * ((n_keys + LANES - 1) // LANES)
    topk = min(TOPK_MAX, n_keys // 4)
    kernel = functools.partial(_dsa_sample_kernel, tq=tq, kb=kb, topk=topk, n_keys=n_keys)
    bmap = lambda b: (b, 0, 0)
    rows_g = HEADS_PER_KV * tq
    return pl.pallas_call(
        kernel,
        grid=(s_total,),
        in_specs=[
            pl.BlockSpec((1, tq, qw), bmap),
            pl.BlockSpec((1, tq, IDX_COLS), bmap),
            pl.BlockSpec((1, tq, LANES), bmap),
            pl.BlockSpec((1, n_past, IDX_DIM), bmap),
            pl.BlockSpec((1, tq, 2 * LANES), bmap),
            pl.BlockSpec((1, N_KV_HEADS * n_past, HEAD_DIM), bmap),
            pl.BlockSpec((1, N_KV_HEADS * tq, HEAD_DIM), bmap),
            pl.BlockSpec((1, N_KV_HEADS * n_past, HEAD_DIM), bmap),
            pl.BlockSpec((1, N_KV_HEADS * tq, HEAD_DIM), bmap),
        ],
        out_specs=pl.BlockSpec((1, tq, qw), bmap),
        out_shape=jax.ShapeDtypeStruct((s_total, tq, qw), BF16),
        scratch_shapes=[
            pltpu.VMEM((1, tq, kb), F32),
            pltpu.VMEM((N_IDX_HEADS * tq, 2 * LANES), BF16),
            pltpu.VMEM((N_IDX_HEADS, tq, LANES), F32),
            pltpu.VMEM((N_KV_HEADS, rows_g, HEAD_DIM), BF16),
            pltpu.VMEM((N_KV_HEADS, rows_g, HEAD_DIM), F32),
            pltpu.VMEM((N_KV_HEADS, rows_g, LANES), F32),
            pltpu.VMEM((N_KV_HEADS, rows_g, LANES), F32),
            pltpu.VMEM((tq, LANES), F32),
            pltpu.VMEM((tq, LANES), F32),
            pltpu.VMEM((tq, LANES), F32),
            pltpu.VMEM((tq, LANES), F32),
            pltpu.VMEM((tq, kb), F32),
            pltpu.VMEM((N_KV_HEADS, rows_g, kb), F32),
            pltpu.VMEM((N_KV_HEADS, rows_g, kb), BF16),
            pltpu.VMEM((N_KV_HEADS, rows_g, LANES), F32),
            pltpu.VMEM((N_IDX_HEADS * tq, kb), F32),
            pltpu.VMEM((kb, 2 * LANES), BF16),
            pltpu.VMEM((N_KV_HEADS, kb, HEAD_DIM), BF16),
            pltpu.VMEM((N_KV_HEADS, kb, HEAD_DIM), BF16),
        ],
        compiler_params=_cparams(1),
        name="dsa_sample",
    )(q, qi, kw, kic, ki4n, kc, kn, vc, vn)


def _dsa_prompt_kernel(q_ref, qi_ref, kw_ref, ki4t_ref, kt_ref, v_ref, o_ref,
                       scr, qi4_scr, wb_scr, qa_scr, ka_scr, bias_scr, acc_scr, m_scr, l_scr,
                       lg_scr, p_scr, corr_scr, s_scr, thr_scr, *, kb, topk):
    tq = LANES
    qb = pl.program_id(1)
    lane_row = lax.broadcasted_iota(jnp.int32, (1, tq), 1)
    n_adm = ((qb * tq + lane_row) // CHUNK + 1) * CHUNK
    nkb = ((qb + 1) * tq + kb - 1) // kb
    lane128 = lax.broadcasted_iota(jnp.int32, (tq, LANES), 1)
    sub128 = lax.broadcasted_iota(jnp.int32, (LANES, tq), 0)
    nt = (((1,), (1,)), ((), ()))
    rows_g = HEADS_PER_KV * tq

    qi = qi_ref[0]
    for h in range(N_IDX_HEADS):
        tile = qi[:, (h // 2) * LANES:(h // 2 + 1) * LANES]
        own = (lane128 < IDX_DIM) if h % 2 == 0 else (lane128 >= IDX_DIM)
        both = jnp.where(own, tile, pltpu.roll(tile, IDX_DIM, 1))
        hi = both.astype(BF16).astype(F32)
        mix = jnp.where(lane128 < IDX_DIM, hi, both - hi).astype(BF16)
        qi4_scr[h * tq:(h + 1) * tq, 0:LANES] = mix
        qi4_scr[h * tq:(h + 1) * tq, LANES:2 * LANES] = mix
        wb_scr[h] = jnp.broadcast_to(kw_ref[0, :, WI_LANE + h:WI_LANE + h + 1], (tq, LANES))
    eye = jnp.where(sub128 == lane128, 1.0, 0.0).astype(BF16)
    for g in range(N_KV_HEADS):
        for r in range(HEADS_PER_KV):
            hh = g * HEADS_PER_KV + r
            qa_scr[g, r * tq:(r + 1) * tq, 0:HEAD_DIM] = (
                q_ref[0, :, hh * HEAD_DIM:(hh + 1) * HEAD_DIM])
            qa_scr[g, r * tq:(r + 1) * tq, HEAD_DIM:HEAD_DIM + tq] = eye

    n_blocks = ki4t_ref.shape[2] // kb

    def head_products(blk, slot):
        kblk = ki4t_ref[0, :, pl.ds(pl.multiple_of(blk * kb, kb), kb)]
        s_scr[slot] = jnp.dot(qi4_scr[...], kblk, preferred_element_type=F32)

    def reduce_heads(blk, slot, carry):
        rmin, rmax = carry
        for c in range(kb // LANES):
            cs = slice(c * LANES, (c + 1) * LANES)
            acc = jnp.zeros((tq, LANES), F32)
            for h in range(N_IDX_HEADS):
                acc = acc + wb_scr[h] * jnp.maximum(s_scr[slot, h * tq:(h + 1) * tq, cs], 0.0)
            acc_t = acc.T
            allowed = (blk * kb + c * LANES + sub128) < n_adm
            scr[blk, cs, :] = jnp.where(allowed, acc_t, NEG)
            rmin = jnp.minimum(rmin, jnp.where(allowed, acc_t, BIG))
            rmax = jnp.maximum(rmax, jnp.where(allowed, acc_t, -BIG))
        return rmin, rmax

    def score_step(i, carry):
        second = jnp.minimum(2 * i + 1, n_blocks - 1)
        head_products(second, 1)
        carry = reduce_heads(2 * i, 0, carry)
        head_products(jnp.minimum(2 * i + 2, n_blocks - 1), 0)
        return reduce_heads(second, 1, carry)

    head_products(0, 0)
    rmin, rmax = lax.fori_loop(
        0, (nkb + 1) // 2, score_step,
        (jnp.full((LANES, tq), BIG, F32), jnp.full((LANES, tq), -BIG, F32)))
    rmin = jnp.min(rmin, axis=0, keepdims=True)
    rmax = jnp.max(rmax, axis=0, keepdims=True)

    def over_tiles(tile_fn, init, combine):
        def body(j, acc):
            for c in range(kb // LANES):
                kpos = (j * kb + c * LANES + sub128).astype(F32)
                acc = combine(acc, tile_fn(scr[j, c * LANES:(c + 1) * LANES, :], kpos))
            return acc

        return lax.fori_loop(0, nkb, body, jnp.full((LANES, tq), init, F32))

    def fold_rows(x, combine):
        n = x.shape[0]
        while n > 8:
            n //= 2
            x = combine(x[:n], x[n:])
        return x

    def count(pred_fn):
        part = fold_rows(over_tiles(pred_fn, 0.0, jnp.add), jnp.add)
        return jnp.sum(part, axis=0, keepdims=True)

    def count_ge(t):
        return count(lambda s, kpos: jnp.where(s >= t, 1.0, 0.0))

    def max_below(bound, strict):
        def tile(s, kpos):
            ok = (s < bound) if strict else (s <= bound)
            return jnp.where(ok, s, -BIG)

        part = fold_rows(over_tiles(tile, -BIG, jnp.maximum), jnp.maximum)
        return jnp.max(part, axis=0, keepdims=True)

    kf = np.float32(topk)
    few_f = np.float32(PEEL_POP)
    need = jnp.where(n_adm > topk, 1.0, 0.0)

    def bisect_pass(state):
        lo, hi, clo, chi, hit, thr = state
        mid = 0.5 * lo + 0.5 * hi
        cnt = count_ge(mid)
        ge = cnt >= kf
        exact = cnt == kf
        new = (jnp.where(ge, mid, lo), jnp.where(ge, hi, mid),
               jnp.where(ge, cnt, clo), jnp.where(ge, chi, cnt),
               jnp.where(exact, 1.0, hit),
               jnp.where(hit > 0.5, thr, jnp.where(exact, mid, thr)))
        moved = jnp.where(new[0] != lo, 1.0, jnp.where(new[1] != hi, 1.0, 0.0))
        few = (new[2] - new[3]) <= few_f
        pending = jnp.where(new[4] > 0.5, 0.0, jnp.where(few, 0.0, moved))
        return new, pending

    above_max = rmax + (jnp.abs(rmax) * 1e-6 + 1e-30)
    state = (rmin, above_max, n_adm.astype(F32), jnp.zeros((1, tq), F32), 1.0 - need, rmin)

    state, pending = lax.fori_loop(0, BISECT_PREFIX, lambda _, c: bisect_pass(c[0]),
                                   (state, jnp.ones((1, tq), F32)))

    def bis_cond(carry):
        return jnp.logical_and(carry[0] < BISECT_CAP, carry[1] > 0.0)

    def bis_body(carry):
        state, _ = bisect_pass(carry[2])
        state, pending = bisect_pass(state)
        return carry[0] + 2, jnp.max(pending), state

    _, _, state = lax.while_loop(bis_cond, bis_body,
                                 (jnp.int32(BISECT_PREFIX), jnp.max(pending), state))
    lo, hi, clo, chi, hit, thr = state

    open_f = jnp.where(hit > 0.5, 0.0, 1.0)
    few = jnp.where((clo - chi) <= few_f, open_f, 0.0) > 0.5
    rank = kf - chi
    t = hi
    for step in range(1, PEEL_POP):
        t = jnp.where(rank >= np.float32(step), max_below(t, True), t)
    peeled = jnp.where(few, jnp.where(count_ge(t) == kf, 1.0, 0.0), 0.0)
    thr = jnp.where(peeled > 0.5, t, thr)
    thr_scr[...] = thr
    tie_f = open_f * (1.0 - peeled)

    @pl.when(jnp.max(tie_f) > 0.0)
    def _ties():
        tie = tie_f > 0.5

        def short(c):
            return jnp.where(tie, jnp.where(c < kf, 1.0, 0.0), 0.0)

        def snap_cond(carry):
            return jnp.max(short(carry[1])) > 0.0

        def snap_body(carry):
            t_old, c_old = carry
            t_new = max_below(t_old, True)
            c_new = count_ge(t_new)
            upd = short(c_old) > 0.5
            return jnp.where(upd, t_new, t_old), jnp.where(upd, c_new, c_old)

        t0 = max_below(hi, False)
        t, _ = lax.while_loop(snap_cond, snap_body, (t0, count_ge(t0)))
        keep = kf - count(lambda s, kpos: jnp.where(s > t, 1.0, 0.0))

        def count_eq_before(cut):
            return count(lambda s, kpos: jnp.where(s == t, jnp.where(kpos < cut, 1.0, 0.0), 0.0))

        def cut_body(_, carry):
            clo, chi = carry
            cmid = jnp.floor(0.5 * (clo + chi))
            ok = count_eq_before(cmid) >= keep
            return jnp.where(ok, clo, cmid), jnp.where(ok, cmid, chi)

        n_total = np.float32(scr.shape[0] * kb)
        steps = int(np.ceil(np.log2(scr.shape[0] * kb))) + 1
        _, cut = lax.fori_loop(0, steps, cut_body, (jnp.zeros((1, tq), F32),
                                                    jnp.full((1, tq), n_total, F32)))
        cut = jnp.where(tie, cut, n_total)

        def drop(j, carry):
            for c in range(kb // LANES):
                rs = slice(c * LANES, (c + 1) * LANES)
                s = scr[j, rs, :]
                kpos = (j * kb + c * LANES + sub128).astype(F32)
                dropped = jnp.where(s == t, jnp.where(kpos >= cut, 1.0, 0.0), 0.0)
                scr[j, rs, :] = jnp.where(dropped > 0.5, NEG, s)
            return carry

        lax.fori_loop(0, nkb, drop, 0)
        thr_scr[...] = jnp.where(tie, t, thr)

    m_scr[...] = jnp.full(m_scr.shape, NEG, F32)
    l_scr[...] = jnp.zeros(l_scr.shape, F32)
    acc_scr[...] = jnp.zeros(acc_scr.shape, F32)
    thr = thr_scr[...]
    rt = ATT_ROW_TILE
    cols = [slice(c * LANES, (c + 1) * LANES) for c in range(kb // LANES)]

    def qk(j, g):
        koff = pl.ds(pl.multiple_of(j * kb, kb), kb)
        ka_scr[g, 0:HEAD_DIM, :] = kt_ref[0, g * HEAD_DIM:(g + 1) * HEAD_DIM, koff]
        if g == 0:
            for cs in cols:
                mask_t = jnp.where(scr[j, cs, :] >= thr, 0.0, NEG)
                mask = mask_t.T.astype(BF16)
                ka_scr[0, HEAD_DIM:HEAD_DIM + tq, cs] = mask
                bias_scr[:, cs] = mask
        else:
            ka_scr[g, HEAD_DIM:HEAD_DIM + tq, :] = bias_scr[...]
        lg_scr[g] = jnp.dot(qa_scr[g], ka_scr[g], preferred_element_type=F32)

    def pv(j, g):
        koff = pl.ds(pl.multiple_of(j * kb, kb), kb)
        vg = v_ref[0, koff, g * HEAD_DIM:(g + 1) * HEAD_DIM]
        acc_scr[g] = acc_scr[g] * corr_scr[g] + jnp.dot(p_scr[g], vg, preferred_element_type=F32)

    def softmax(g):
        for i in range(rows_g // rt):
            rs = slice(i * rt, (i + 1) * rt)
            lg = [lg_scr[g, rs, cs] for cs in cols]
            blk_max = jnp.max(functools.reduce(jnp.maximum, lg), axis=1, keepdims=True)
            m_old = m_scr[g, rs, :]
            m_new = jnp.maximum(m_old, jnp.broadcast_to(blk_max, (rt, LANES)))
            corr = jnp.exp2(m_old - m_new)
            p_sum = jnp.zeros((rt, LANES), F32)
            for cs, lg_c in zip(cols, lg):
                p = jnp.exp2(lg_c - m_new)
                p_scr[g, rs, cs] = p.astype(BF16)
                p_sum = p_sum + p
            l_scr[g, rs, :] = corr * l_scr[g, rs, :] + p_sum
            corr_scr[g, rs, :] = corr
            m_scr[g, rs, :] = m_new

    def attend(j, carry):
        pv(jnp.maximum(j - 1, 0), 1)
        qk(j, 1)
        softmax(0)
        pv(j, 0)
        qk(jnp.minimum(j + 1, nkb - 1), 0)
        softmax(1)
        return carry

    p_scr[1] = jnp.zeros(p_scr.shape[1:], BF16)
    corr_scr[1] = jnp.zeros(corr_scr.shape[1:], F32)
    qk(0, 0)
    lax.fori_loop(0, nkb, attend, 0)
    pv(nkb - 1, 1)
    for g in range(N_KV_HEADS):
        o = acc_scr[g] / jnp.sum(l_scr[g], axis=1, keepdims=True)
        for r in range(HEADS_PER_KV):
            hh = g * HEADS_PER_KV + r
            o_ref[0, :, hh * HEAD_DIM:(hh + 1) * HEAD_DIM] = o[r * tq:(r + 1) * tq].astype(o_ref.dtype)


def _dsa_prompt(q, qi, kw, ki4t, kt, v_bf, *, kb):
    s_total, t_q, qw = q.shape
    tq = LANES
    lp = ki4t.shape[2]
    assert lp == t_q and t_q % tq == 0 and lp % kb == 0 and kb % (2 * LANES) == 0
    topk = min(TOPK_MAX, lp // 4)
    kernel = functools.partial(_dsa_prompt_kernel, kb=kb, topk=topk)
    qmap = lambda b, j: (b, j, 0)
    kmap = lambda b, j: (b, 0, 0)
    rows_g = HEADS_PER_KV * tq
    stat = pltpu.VMEM((N_KV_HEADS, rows_g, LANES), F32)
    return pl.pallas_call(
        kernel,
        grid=(s_total, t_q // tq),
        in_specs=[
            pl.BlockSpec((1, tq, qw), qmap),
            pl.BlockSpec((1, tq, IDX_COLS), qmap),
            pl.BlockSpec((1, tq, LANES), qmap),
            pl.BlockSpec((1, 2 * LANES, lp), kmap, pipeline_mode=pl.Buffered(1)),
            pl.BlockSpec((1, kt.shape[1], lp), kmap, pipeline_mode=pl.Buffered(1)),
            pl.BlockSpec((1, lp, v_bf.shape[2]), kmap, pipeline_mode=pl.Buffered(1)),
        ],
        out_specs=pl.BlockSpec((1, tq, qw), qmap),
        out_shape=jax.ShapeDtypeStruct((s_total, t_q, qw), BF16),
        scratch_shapes=[
            pltpu.VMEM((lp // kb, kb, tq), F32),
            pltpu.VMEM((N_IDX_HEADS * tq, 2 * LANES), BF16),
            pltpu.VMEM((N_IDX_HEADS, tq, LANES), F32),
            pltpu.VMEM((N_KV_HEADS, rows_g, HEAD_DIM + tq), BF16),
            pltpu.VMEM((N_KV_HEADS, HEAD_DIM + tq, kb), BF16),
            pltpu.VMEM((tq, kb), BF16),
            pltpu.VMEM((N_KV_HEADS, rows_g, HEAD_DIM), F32),
            stat, stat,
            pltpu.VMEM((N_KV_HEADS, rows_g, kb), F32),
            pltpu.VMEM((N_KV_HEADS, rows_g, kb), BF16),
            stat,
            pltpu.VMEM((2, N_IDX_HEADS * tq, kb), F32),
            pltpu.VMEM((1, tq), F32),
        ],
        compiler_params=_cparams(2),
        name="dsa_prompt",
    )(q, qi, kw, ki4t, kt, v_bf)


def _mix_kernel(x_ref, gt_ref, gpost_ref, att_ref, u_ref, vn_ref, gates_ref, ws_ref, bs_ref,
                wpa_ref, wpb_ref, wout_ref, o_ref, sgu_scr):
    nb, t, d = x_ref.shape
    rows = nb * t
    gd = d // N_SGU_GROUPS
    dot = functools.partial(jnp.dot, preferred_element_type=F32)
    r = lax.broadcasted_iota(jnp.int32, (MLP_CHUNK, MLP_CHUNK), 0)
    c = lax.broadcasted_iota(jnp.int32, (MLP_CHUNK, MLP_CHUNK), 1)
    for g in range(N_SGU_GROUPS):
        wg = jnp.where(r >= c, ws_ref[g], 0.0).astype(BF16)
        gs = slice(g * gd, (g + 1) * gd)
        for ch in range(rows // MLP_CHUNK):
            rs = slice(ch * MLP_CHUNK, (ch + 1) * MLP_CHUNK)
            mixed = dot(wg, vn_ref[rs, gs].astype(BF16)) + bs_ref[:, gs]
            sgu_scr[rs, gs] = (u_ref[rs, gs].astype(F32) * mixed).astype(BF16)
    a = dot(att_ref[...], wpa_ref[...])
    b = dot(sgu_scr[...], wpb_ref[...])
    merged = gates_ref[:, 0:d].astype(F32) * a + gates_ref[:, d:2 * d].astype(F32) * b
    h = dot(merged.astype(BF16), wout_ref[...])
    hn = _rms(h, gpost_ref[...])
    o_ref[...] = x_ref[...] + gt_ref[...] * hn.reshape(nb, t, d)


def _row_blocking(s_total, t_full, nb):
    if nb == 1:
        t = min(t_full, 512)
        grid = (s_total, t_full // t)
        xmap = lambda b, j: (b, j, 0)
        mmap = lambda b, j: (b, 0, 0)
        rmap = lambda b, j: (b * (t_full // t) + j, 0)
    else:
        t = t_full
        grid = (s_total // nb,)
        xmap = lambda b: (b, 0, 0)
        mmap = lambda b: (b, 0, 0)
        rmap = lambda b: (b, 0)
    return t, grid, xmap, mmap, rmap


def _mix(x, gt, gpost, att, u, vn, gates, ws, bs, wpa, wpb, wout, nb):
    s_total, t_full, d = x.shape
    t, grid, xmap, mmap, rmap = _row_blocking(s_total, t_full, nb)
    rows = nb * t
    return pl.pallas_call(
        _mix_kernel,
        grid=grid,
        in_specs=[
            pl.BlockSpec((nb, t, d), xmap),
            pl.BlockSpec((nb, 1, d), mmap),
            _const_spec((1, 1, d)),
            pl.BlockSpec((rows, d), rmap),
            pl.BlockSpec((rows, d), rmap),
            pl.BlockSpec((rows, d), rmap),
            pl.BlockSpec((rows, 2 * d), rmap),
            _const_spec(ws.shape),
            _const_spec(bs.shape),
            _const_spec(wpa.shape),
            _const_spec(wpb.shape),
            _const_spec(wout.shape),
        ],
        out_specs=pl.BlockSpec((nb, t, d), xmap),
        out_shape=jax.ShapeDtypeStruct(x.shape, F32),
        scratch_shapes=[pltpu.VMEM((rows, d), BF16)],
        compiler_params=_cparams(len(grid)),
        name="mix",
    )(x, gt, gpost.reshape(1, 1, d), att, u, vn, gates, ws, bs, wpa, wpb, wout)


def _ffn_kernel(x_ref, sh_ref, sc_ref, gt_ref, gpre_ref, gpost_ref, win_ref, wout_ref, o_ref,
                *, n_split):
    nb, t, d = x_ref.shape
    rows = nb * t
    hidden = wout_ref.shape[0]
    hc = hidden // n_split
    dot = functools.partial(jnp.dot, preferred_element_type=F32)
    x = x_ref[...]
    xn = (_rms(x, gpre_ref[...]) * (1.0 + sc_ref[...]) + sh_ref[...]).reshape(rows, d).astype(BF16)
    acc = jnp.zeros((rows, d), F32)
    for i in range(n_split):
        gg = dot(xn, win_ref[:, i * hc:(i + 1) * hc])
        up = dot(xn, win_ref[:, hidden + i * hc:hidden + (i + 1) * hc])
        hmid = (gg * jax.nn.sigmoid(gg) * up).astype(BF16)
        acc = acc + dot(hmid, wout_ref[i * hc:(i + 1) * hc, :])
    hn = _rms(acc, gpost_ref[...])
    o_ref[...] = x + gt_ref[...] * hn.reshape(nb, t, d)


def _ffn(x, sh, sc, gt, gpre, gpost, win, wout, nb):
    s_total, t_full, d = x.shape
    t, grid, xmap, mmap, _ = _row_blocking(s_total, t_full, nb)
    hidden = wout.shape[0]
    n_split = 2
    assert hidden % (n_split * LANES) == 0
    return pl.pallas_call(
        functools.partial(_ffn_kernel, n_split=n_split),
        grid=grid,
        in_specs=[
            pl.BlockSpec((nb, t, d), xmap),
            pl.BlockSpec((nb, 1, d), mmap),
            pl.BlockSpec((nb, 1, d), mmap),
            pl.BlockSpec((nb, 1, d), mmap),
            _const_spec((1, 1, d)),
            _const_spec((1, 1, d)),
            _const_spec(win.shape),
            _const_spec(wout.shape),
        ],
        out_specs=pl.BlockSpec((nb, t, d), xmap),
        out_shape=jax.ShapeDtypeStruct(x.shape, F32),
        compiler_params=_cparams(len(grid)),
        name="ffn",
    )(x, sh, sc, gt, gpre.reshape(1, 1, d), gpost.reshape(1, 1, d), win, wout)


def _prep_weights(w_in, w_proj_a, w_proj_b, w_out, w_ffn_in, w_ffn_out, d):
    qw, kvw = N_HEADS * HEAD_DIM, N_KV_HEADS * HEAD_DIM
    o_q, o_k, o_v = 0, qw, qw + kvw
    o_qi = qw + 2 * kvw
    o_ki = o_qi + IDX_COLS
    o_wi = o_ki + IDX_DIM
    o_u = o_wi + N_IDX_HEADS
    wmain = jnp.concatenate([w_in[:, o_q:o_qi], w_in[:, o_u:]], axis=1).astype(BF16)
    pad = jnp.zeros((d, IDX_PAD - IDX_COLS - IDX_DIM - N_IDX_HEADS), F32)
    widx = jnp.concatenate([w_in[:, o_qi:o_u], pad], axis=1)
    wih, wil = _split_bf16(widx)
    return dict(
        wmain=wmain, wih=wih, wil=wil,
        wpa=w_proj_a.astype(BF16), wpb=w_proj_b.astype(BF16), wout=w_out.astype(BF16),
        wfi=w_ffn_in.astype(BF16), wfo=w_ffn_out.astype(BF16),
    )


def _spatial_tables(w_spatial, b_spatial, seq_len, d):
    cl = min(seq_len, MLP_CHUNK)
    reps = MLP_CHUNK // cl
    gd = d // N_SGU_GROUPS
    w = w_spatial[:, :cl, :cl]
    if reps > 1:
        eye = jnp.eye(reps, dtype=w.dtype)
        w = jnp.einsum("ab,gts->gatbs", eye, w).reshape(N_SGU_GROUPS, MLP_CHUNK, MLP_CHUNK)
    b = jnp.tile(b_spatial[:, :cl].T, (reps, 1))
    b = jnp.repeat(b, gd, axis=1)
    return w, b


def _stream(x, mod, pos, wts, g_pre_mix, g_post_mix, g_pre_ffn, g_post_ffn, b_gate, g_sgu,
            w_spatial, b_spatial, past, nb):
    s_total, t_full, d = x.shape
    sh1, sc1, gt1, sh2, sc2, gt2 = [m[:, None, :] for m in jnp.split(mod, 6, axis=-1)]

    idx_scale = np.float32((IDX_DIM * N_IDX_HEADS) ** -0.5)
    kw_scale = jnp.ones((LANES,), F32).at[WI_LANE:WI_LANE + N_IDX_HEADS].set(idx_scale)
    if nb == 1:
        tpos = pos
        tab_map = lambda j: (j, 0)
    else:
        tpos = jnp.tile(pos, nb)
        tab_map = lambda j: (0, 0)
    tabs = (_rope_tables(tpos, HEAD_DIM), _rope_tables(tpos, IDX_DIM), _kw_tables(tpos, kw_scale))

    vn_dtype = F32 if past is not None else BF16
    (q, k, v, k_bf, v_bf, qi, kw, ki4, u, vn, gates) = _proj(
        x, sh1, sc1, g_pre_mix, wts["wmain"], wts["wih"], wts["wil"], tabs, tab_map,
        b_gate, g_sgu, nb, vn_dtype)

    shp = lambda a: a.reshape(s_total, t_full, a.shape[-1])
    if past is None:
        n_keys = t_full
        key_major = lambda a: jnp.swapaxes(shp(a), 1, 2)
        att = _dsa_prompt(shp(q), shp(qi), shp(kw), key_major(ki4), key_major(k_bf), shp(v_bf),
                          kb=min(DSA_KEY_BLOCK, t_full))
    else:
        pk, pv, pki = past
        rows_view = lambda a: a.reshape(s_total, -1, HEAD_DIM)
        att = _dsa_sample(shp(q), shp(qi), shp(kw), pki, shp(ki4), rows_view(pk), rows_view(k),
                          rows_view(pv), rows_view(v))
    att = att.reshape(s_total * t_full, -1)

    ws, bs = _spatial_tables(w_spatial, b_spatial, t_full, d)
    x1 = _mix(x, gt1, g_post_mix, att, u, vn, gates, ws, bs,
              wts["wpa"], wts["wpb"], wts["wout"], nb)
    x2 = _ffn(x1, sh2, sc2, gt2, g_pre_ffn, g_post_ffn, wts["wfi"], wts["wfo"], nb)
    heads = lambda a: a.reshape(s_total, t_full, N_KV_HEADS, HEAD_DIM)
    return x2, heads(k), heads(v), shp(kw)[..., :IDX_DIM], shp(vn)


def _kw_tables(pos, scale_lanes):
    tab = _rope_tables(pos, IDX_DIM)
    lane = jnp.arange(LANES)
    c = jnp.where(lane[None, :] < IDX_DIM, tab[0], 1.0) * scale_lanes[None, :]
    s = jnp.where(lane[None, :] < IDX_DIM, tab[1], 0.0)
    return jnp.stack([c, s])


def kernel(x_prompt, x_sample, cache_k, cache_v, cache_idx_k, c_prompt, c_sample, w_ada, b_ada,
           g_pre_mix, g_post_mix, g_pre_ffn, g_post_ffn, w_in, b_gate, g_sgu, w_spatial,
           b_spatial, w_proj_a, w_proj_b, w_out, w_ffn_in, w_ffn_out):
    depth = w_ada.shape[0]
    bp, t_p, d = x_prompt.shape
    bs_, t_s, _ = x_sample.shape
    past_len = cache_k.shape[2]
    pos_p = jnp.arange(t_p, dtype=jnp.int32)
    pos_s = past_len + jnp.arange(t_s, dtype=jnp.int32)
    nb_s = max(1, min(bs_, 256 // t_s))

    xp, xs = x_prompt, x_sample
    outs = [[] for _ in range(7)]
    for l in range(depth):
        wts = _prep_weights(w_in[l], w_proj_a[l], w_proj_b[l], w_out[l], w_ffn_in[l],
                            w_ffn_out[l], d)
        c_all = jnp.concatenate([c_prompt, c_sample], axis=0)
        pad_rows = (-c_all.shape[0]) % 8
        c_all = jnp.pad(c_all, ((0, pad_rows), (0, 0)))
        mod = _ada(c_all, w_ada[l], b_ada[l])
        common = (wts, g_pre_mix[l], g_post_mix[l], g_pre_ffn[l], g_post_ffn[l], b_gate[l],
                  g_sgu[l], w_spatial[l], b_spatial[l])
        xp, kp, vp, kip, _ = _stream(xp, mod[:bp], pos_p, *common, None, 1)
        past = (cache_k[l], cache_v[l], cache_idx_k[l])
        xs, ks, vs, kis, vns = _stream(xs, mod[bp:bp + bs_], pos_s, *common, past, nb_s)
        for lst, val in zip(outs, (kp, vp, kip, ks, vs, kis, vns)):
            lst.append(val)
    return (xp, xs) + tuple(jnp.stack(o) for o in outs)
```

```python
import functools

import jax
import jax.numpy as jnp
import numpy as np
from jax import lax
from jax.experimental import pallas as pl
from jax.experimental.pallas import tpu as pltpu

F32 = jnp.float32
BF16 = jnp.bfloat16

CHUNK = 64
N_HEADS = 8
N_KV_HEADS = 2
HEAD_DIM = 128
N_IDX_HEADS = 8
IDX_DIM = 64
TOPK_MAX = 256
ROPE_THETA = 500000.0
ROT_FRACTION = 4
MLP_CHUNK = 128
N_SGU_GROUPS = 8
NEG = -1e30
BIG = 3e38
EPS = 1e-6

LANES = 128
VMEM_LIMIT_BYTES = 56 * 1024 * 1024

HEADS_PER_KV = N_HEADS // N_KV_HEADS
IDX_COLS = N_IDX_HEADS * IDX_DIM
IDX_PAD = IDX_COLS + LANES
WI_LANE = IDX_DIM
DSA_KEY_BLOCK = 512
ATT_ROW_TILE = 32
LOG2E = float(np.log2(np.e))
BISECT_PREFIX = 12
PEEL_POP = 4
BISECT_CAP = 40


def _cparams(n_axes):
    return pltpu.CompilerParams(
        dimension_semantics=("arbitrary",) * n_axes,
        vmem_limit_bytes=VMEM_LIMIT_BYTES,
    )


def _const_spec(shape):
    nd = len(shape)
    return pl.BlockSpec(shape, lambda *_: (0,) * nd, pipeline_mode=pl.Buffered(1))


def _split_bf16(x):
    hi = x.astype(BF16)
    lo = (x - hi.astype(F32)).astype(BF16)
    return hi, lo


def _gelu(x):
    c = np.float32(np.sqrt(2.0 / np.pi))
    return 0.5 * x * (1.0 + jnp.tanh(c * (x + 0.044715 * (x * x * x))))


def _rms(x, g):
    return x * lax.rsqrt(jnp.mean(x * x, axis=-1, keepdims=True) + EPS) * g


def _ada_kernel(c_ref, w_ref, b_ref, o_ref):
    c = c_ref[...]
    a = c * jax.nn.sigmoid(c)
    ah, al = _split_bf16(a)
    wh, wl = _split_bf16(w_ref[...])
    dot = functools.partial(jnp.dot, preferred_element_type=F32)
    o_ref[...] = dot(ah, wh) + dot(al, wh) + dot(ah, wl) + b_ref[...]


def _ada(c, w_ada, b_ada):
    rows, d = c.shape
    n = w_ada.shape[1]
    tn = 1024
    return pl.pallas_call(
        _ada_kernel,
        grid=(n // tn,),
        in_specs=[
            pl.BlockSpec((rows, d), lambda j: (0, 0)),
            pl.BlockSpec((d, tn), lambda j: (0, j)),
            pl.BlockSpec((1, tn), lambda j: (0, j)),
        ],
        out_specs=pl.BlockSpec((rows, tn), lambda j: (0, j)),
        out_shape=jax.ShapeDtypeStruct((rows, n), F32),
        compiler_params=_cparams(1),
        name="ada",
    )(c, w_ada, b_ada.reshape(1, n))


def _rope_tables(pos, head_dim):
    rd = head_dim // ROT_FRACTION
    half = rd // 2
    inv = jnp.power(ROPE_THETA, -(jnp.arange(half, dtype=F32) / half))
    ang = pos.astype(F32)[:, None] * inv[None, :]
    cos, sin = jnp.cos(ang), jnp.sin(ang)
    t = pos.shape[0]
    ones = jnp.ones((t, head_dim - rd), F32)
    zeros = jnp.zeros((t, head_dim - rd), F32)
    c = jnp.concatenate([cos, cos, ones], axis=-1)
    s = jnp.concatenate([-sin, sin, zeros], axis=-1)
    reps = LANES // head_dim
    c, s = jnp.tile(c, (1, reps)), jnp.tile(s, (1, reps))
    return jnp.stack([c, s])


def _rope_apply(x, tab_ref, half):
    lane = lax.broadcasted_iota(jnp.int32, x.shape, 1)
    fwd = pltpu.roll(x, LANES - half, 1)
    bwd = pltpu.roll(x, half, 1)
    partner = jnp.where((lane % (2 * half)) < half, fwd, bwd)
    return x * tab_ref[0] + partner * tab_ref[1]


def _proj_kernel(x_ref, sh_ref, sc_ref, gpre_ref, wmain_ref, wih_ref, wil_ref,
                 tq_ref, ti_ref, tk_ref, bgate_ref, gsgu_ref,
                 q_ref, k_ref, v_ref, kbf_ref, vbf_ref, qi_ref, kw_ref, ki4_ref,
                 u_ref, vn_ref, gates_ref):
    nb, t, d = x_ref.shape
    rows = nb * t
    xn = _rms(x_ref[...], gpre_ref[...]) * (1.0 + sc_ref[...]) + sh_ref[...]
    xn = xn.reshape(rows, d)
    xh, xl = _split_bf16(xn)
    dot = functools.partial(jnp.dot, preferred_element_type=F32)
    q_scale = np.float32(HEAD_DIM ** -0.5 * LOG2E)
    qw = N_HEADS * HEAD_DIM
    kvw = N_KV_HEADS * HEAD_DIM
    half_h = HEAD_DIM // ROT_FRACTION // 2
    half_i = IDX_DIM // ROT_FRACTION // 2

    y = dot(xh, wmain_ref[:, 0:qw])
    for h in range(N_HEADS):
        sl = slice(h * HEAD_DIM, (h + 1) * HEAD_DIM)
        q_ref[:, sl] = (_rope_apply(y[:, sl], tq_ref, half_h) * q_scale).astype(BF16)

    y = dot(xh, wmain_ref[:, qw:qw + kvw])
    for h in range(N_KV_HEADS):
        sl = slice(h * HEAD_DIM, (h + 1) * HEAD_DIM)
        kr = _rope_apply(y[:, sl], tq_ref, half_h)
        k_ref[pl.ds(h, rows, stride=N_KV_HEADS), :] = kr
        kbf_ref[:, sl] = kr.astype(BF16)
    y = dot(xh, wmain_ref[:, qw + kvw:qw + 2 * kvw])
    for h in range(N_KV_HEADS):
        v_ref[pl.ds(h, rows, stride=N_KV_HEADS), :] = y[:, h * HEAD_DIM:(h + 1) * HEAD_DIM]
    vbf_ref[...] = y.astype(BF16)

    wih = wih_ref[...]
    yi = dot(xh, wih) + dot(xl, wih) + dot(xh, wil_ref[...])
    for tile in range(IDX_COLS // LANES):
        sl = slice(tile * LANES, (tile + 1) * LANES)
        qi_ref[:, sl] = _rope_apply(yi[:, sl], ti_ref, half_i)
    kw = _rope_apply(yi[:, IDX_COLS:IDX_PAD], tk_ref, half_i)
    kw_ref[...] = kw
    lane = lax.broadcasted_iota(jnp.int32, kw.shape, 1)
    kk = jnp.where(lane < IDX_DIM, kw, pltpu.roll(kw, IDX_DIM, 1))
    kh, kl = _split_bf16(kk)
    ki4_ref[:, 0:LANES] = kh
    ki4_ref[:, LANES:2 * LANES] = kl

    base = qw + 2 * kvw
    u_ref[...] = _gelu(dot(xh, wmain_ref[:, base:base + d])).astype(u_ref.dtype)
    g = _gelu(dot(xh, wmain_ref[:, base + d:base + 2 * d]))
    mu = jnp.mean(g, axis=-1, keepdims=True)
    gc = g - mu
    vn = gc * lax.rsqrt(jnp.mean(gc * gc, axis=-1, keepdims=True) + EPS) * gsgu_ref[...]
    vn_ref[...] = vn.astype(vn_ref.dtype)

    y = dot(xh, wmain_ref[:, base + 2 * d:base + 4 * d]) + bgate_ref[...]
    gates_ref[...] = jax.nn.sigmoid(y).astype(BF16)


def _proj(x, sh, sc, gpre, wmain, wih, wil, tabs, tab_map, bgate, gsgu, nb, vn_dtype):
    s_total, t_full, d = x.shape
    tq, ti, tk = tabs
    if nb == 1:
        t = min(t_full, 512)
        grid = (s_total, t_full // t)
        xmap = lambda b, j: (b, j, 0)
        mmap = lambda b, j: (b, 0, 0)
        rmap = lambda b, j: (b * (t_full // t) + j, 0)
        tmap = lambda b, j: (0,) + tab_map(j)
    else:
        t = t_full
        grid = (s_total // nb,)
        xmap = lambda b: (b, 0, 0)
        mmap = lambda b: (b, 0, 0)
        rmap = lambda b: (b, 0)
        tmap = lambda b: (0,) + tab_map(0)
    rows = nb * t
    n_rows = s_total * t_full
    qw, kvw = N_HEADS * HEAD_DIM, N_KV_HEADS * HEAD_DIM

    def row_spec(width, per_token=1):
        return pl.BlockSpec((rows * per_token, width), rmap)

    def row_shape(width, dtype, per_token=1):
        return jax.ShapeDtypeStruct((n_rows * per_token, width), dtype)

    tab_spec = pl.BlockSpec((2, rows, LANES), tmap)
    outs = [(qw, BF16), (HEAD_DIM, F32, N_KV_HEADS), (HEAD_DIM, F32, N_KV_HEADS), (kvw, BF16),
            (kvw, BF16), (IDX_COLS, F32), (LANES, F32), (2 * LANES, BF16), (d, BF16),
            (d, vn_dtype), (2 * d, BF16)]
    return pl.pallas_call(
        _proj_kernel,
        grid=grid,
        in_specs=[
            pl.BlockSpec((nb, t, d), xmap),
            pl.BlockSpec((nb, 1, d), mmap),
            pl.BlockSpec((nb, 1, d), mmap),
            _const_spec((1, 1, d)),
            _const_spec(wmain.shape),
            _const_spec(wih.shape),
            _const_spec(wil.shape),
            tab_spec, tab_spec, tab_spec,
            _const_spec((1, 2 * d)),
            _const_spec((1, d)),
        ],
        out_specs=[row_spec(o[0], *o[2:]) for o in outs],
        out_shape=[row_shape(*o) for o in outs],
        compiler_params=_cparams(len(grid)),
        name="proj",
    )(x, sh, sc, gpre.reshape(1, 1, d), wmain, wih, wil, tq, ti, tk,
      bgate.reshape(1, 2 * d), gsgu.reshape(1, d))


def _dsa_sample_kernel(q_ref, qi_ref, kw_ref, kic_ref, ki4n_ref, kc_ref, kn_ref, vc_ref, vn_ref,
                       o_ref, scr, qi4_scr, wb_scr, qg_scr, acc_scr, m_scr, l_scr, thr_scr, lo_scr,
                       hi_scr, hit_scr, bias_scr, lg_scr, p_scr, corr_scr, s_scr, ki4_scr, k_scr,
                       v_scr, *, tq, kb, topk, n_keys):
    n_past = kic_ref.shape[1]
    n_adm = jnp.full((tq, 1), n_keys, jnp.int32)
    nkb = 1
    lane128 = lax.broadcasted_iota(jnp.int32, (tq, LANES), 1)
    nt = (((1,), (1,)), ((), ()))
    dot = functools.partial(jnp.dot, preferred_element_type=F32)

    sub = lax.broadcasted_iota(jnp.int32, (IDX_DIM, LANES), 0)
    lane = lax.broadcasted_iota(jnp.int32, (IDX_DIM, LANES), 1)
    dup = jnp.where(sub == lane % IDX_DIM, 1.0, 0.0).astype(BF16)
    kih, kil = _split_bf16(kic_ref[0])
    ki4_scr[0:n_past, 0:LANES] = dot(kih, dup).astype(BF16)
    ki4_scr[0:n_past, LANES:2 * LANES] = dot(kil, dup).astype(BF16)
    ki4_scr[n_past:n_keys, :] = ki4n_ref[0]
    ki4_scr[n_keys:kb, :] = jnp.zeros((kb - n_keys, 2 * LANES), BF16)
    for g in range(N_KV_HEADS):
        for dst, cache, new in ((k_scr, kc_ref, kn_ref), (v_scr, vc_ref, vn_ref)):
            dst[g, 0:n_past, :] = cache[0, pl.ds(g, n_past, stride=N_KV_HEADS), :].astype(BF16)
            dst[g, n_past:n_keys, :] = new[0, pl.ds(g, tq, stride=N_KV_HEADS), :].astype(BF16)
            dst[g, n_keys:kb, :] = jnp.zeros((kb - n_keys, HEAD_DIM), BF16)

    qi = qi_ref[0]
    for h in range(N_IDX_HEADS):
        tile = qi[:, (h // 2) * LANES:(h // 2 + 1) * LANES]
        own = (lane128 < IDX_DIM) if h % 2 == 0 else (lane128 >= IDX_DIM)
        both = jnp.where(own, tile, pltpu.roll(tile, IDX_DIM, 1))
        hi = both.astype(BF16).astype(F32)
        mix = jnp.where(lane128 < IDX_DIM, hi, both - hi).astype(BF16)
        qi4_scr[h * tq:(h + 1) * tq, 0:LANES] = mix
        qi4_scr[h * tq:(h + 1) * tq, LANES:2 * LANES] = mix
        wb_scr[h] = jnp.broadcast_to(kw_ref[0, :, WI_LANE + h:WI_LANE + h + 1], (tq, LANES))
    for g in range(N_KV_HEADS):
        for r in range(HEADS_PER_KV):
            hh = g * HEADS_PER_KV + r
            qg_scr[g, r * tq:(r + 1) * tq, :] = q_ref[0, :, hh * HEAD_DIM:(hh + 1) * HEAD_DIM]

    s_scr[...] = lax.dot_general(qi4_scr[...], ki4_scr[...], nt, preferred_element_type=F32)
    rmin = jnp.full((tq, LANES), BIG, F32)
    rmax = jnp.full((tq, LANES), -BIG, F32)
    for c in range(kb // LANES):
        cs = slice(c * LANES, (c + 1) * LANES)
        acc = jnp.zeros((tq, LANES), F32)
        for h in range(N_IDX_HEADS):
            acc = acc + wb_scr[h] * jnp.maximum(s_scr[h * tq:(h + 1) * tq, cs], 0.0)
        allowed = (c * LANES + lane128) < n_adm
        scr[0, :, cs] = jnp.where(allowed, acc, NEG)
        rmin = jnp.minimum(rmin, jnp.where(allowed, acc, BIG))
        rmax = jnp.maximum(rmax, jnp.where(allowed, acc, -BIG))
    rmin = jnp.min(rmin, axis=1, keepdims=True)
    rmax = jnp.max(rmax, axis=1, keepdims=True)

    def rep(col):
        return jnp.broadcast_to(col, (tq, LANES))

    def over_tiles(tile_fn, init, combine):
        def body(j, acc):
            for c in range(kb // LANES):
                kpos = (j * kb + c * LANES + lane128).astype(F32)
                acc = combine(acc, tile_fn(scr[j, :, c * LANES:(c + 1) * LANES], kpos))
            return acc

        return lax.fori_loop(0, nkb, body, jnp.full((tq, LANES), init, F32))

    def count(pred_fn):
        acc = over_tiles(lambda s, kpos: pred_fn(s, kpos), 0.0, jnp.add)
        return rep(jnp.sum(acc, axis=1, keepdims=True))

    def count_ge(t):
        return count(lambda s, kpos: jnp.where(s >= t, 1.0, 0.0))

    kf = np.float32(topk)
    rmin, rmax = rep(rmin), rep(rmax)
    need = rep(jnp.where(n_adm > topk, 1.0, 0.0))
    lo_scr[...] = rmin
    hi_scr[...] = jnp.where(need > 0.5, rmax, rmin)
    hit_scr[...] = jnp.zeros((tq, LANES), F32)

    def bis_cond(carry):
        it, n_moved = carry
        return jnp.logical_and(it < BISECT_CAP, n_moved > 0.0)

    def bis_body(carry):
        it, _ = carry
        lo, hi = lo_scr[...], hi_scr[...]
        mid = 0.5 * lo + 0.5 * hi
        cnt = count_ge(mid)
        new_lo = jnp.where(cnt >= kf, mid, lo)
        new_hi = jnp.where(cnt > kf, hi, mid)
        lo_scr[...] = new_lo
        hi_scr[...] = new_hi
        hit_scr[...] = jnp.where(cnt == kf, 1.0, hit_scr[...])
        moved = jnp.where(new_lo != lo, 1.0, jnp.where(new_hi != hi, 1.0, 0.0))
        return it + 1, jnp.sum(moved)

    first = lax.fori_loop(0, BISECT_PREFIX, lambda _, c: bis_body(c),
                          (jnp.int32(0), jnp.float32(1.0)))
    lax.while_loop(bis_cond, bis_body, first)
    thr_scr[...] = lo_scr[...]
    tie_f = jnp.where(hit_scr[...] > 0.5, 0.0, need)

    @pl.when(jnp.sum(tie_f) > 0.0)
    def _ties():
        tie = tie_f > 0.5

        def max_below(bound, strict):
            def tile(s, kpos):
                ok = (s < bound) if strict else (s <= bound)
                return jnp.where(ok, s, -BIG)

            return rep(jnp.max(over_tiles(tile, -BIG, jnp.maximum), axis=1, keepdims=True))

        def short(c):
            return jnp.logical_and(tie, c < kf)

        def snap_cond(carry):
            return jnp.sum(jnp.where(short(carry[1]), 1.0, 0.0)) > 0.0

        def snap_body(carry):
            t_old, c_old = carry
            t_new = max_below(t_old, True)
            c_new = count_ge(t_new)
            upd = short(c_old)
            return jnp.where(upd, t_new, t_old), jnp.where(upd, c_new, c_old)

        t0 = max_below(hi_scr[...], False)
        t, _ = lax.while_loop(snap_cond, snap_body, (t0, count_ge(t0)))
        keep = kf - count(lambda s, kpos: jnp.where(s > t, 1.0, 0.0))

        def count_eq_before(cut):
            return count(lambda s, kpos: jnp.where(s == t, jnp.where(kpos < cut, 1.0, 0.0), 0.0))

        def cut_body(_, carry):
            clo, chi = carry
            cmid = jnp.floor(0.5 * (clo + chi))
            ok = count_eq_before(cmid) >= keep
            return jnp.where(ok, clo, cmid), jnp.where(ok, cmid, chi)

        n_total = np.float32(scr.shape[0] * kb)
        steps = int(np.ceil(np.log2(scr.shape[0] * kb))) + 1
        _, cut = lax.fori_loop(0, steps, cut_body, (jnp.zeros((tq, LANES), F32),
                                                    jnp.full((tq, LANES), n_total, F32)))
        cut = jnp.where(tie, cut, n_total)

        def drop(j, carry):
            for c in range(kb // LANES):
                cs = slice(c * LANES, (c + 1) * LANES)
                s = scr[j, :, cs]
                kpos = (j * kb + c * LANES + lane128).astype(F32)
                dropped = jnp.where(s == t, jnp.where(kpos >= cut, 1.0, 0.0), 0.0)
                scr[j, :, cs] = jnp.where(dropped > 0.5, NEG, s)
            return carry

        lax.fori_loop(0, nkb, drop, 0)
        thr_scr[...] = jnp.where(tie, t, thr_scr[...])

    m_scr[...] = jnp.full(m_scr.shape, NEG, F32)
    l_scr[...] = jnp.zeros(l_scr.shape, F32)
    acc_scr[...] = jnp.zeros(acc_scr.shape, F32)
    thr = thr_scr[...]

    rows_g = HEADS_PER_KV * tq
    rt = min(tq, ATT_ROW_TILE)
    cols = [slice(c * LANES, (c + 1) * LANES) for c in range(kb // LANES)]

    def qk(g):
        lg_scr[g] = lax.dot_general(qg_scr[g], k_scr[g], nt, preferred_element_type=F32)

    def pv(g):
        acc_scr[g] = acc_scr[g] * corr_scr[g] + dot(p_scr[g], v_scr[g])

    def softmax(g):
        for i in range(rows_g // rt):
            rs = slice(i * rt, (i + 1) * rt)
            qs = slice((i * rt) % tq, (i * rt) % tq + rt)
            lg = [lg_scr[g, rs, cs] + bias_scr[qs, cs] for cs in cols]
            blk_max = jnp.max(functools.reduce(jnp.maximum, lg), axis=1, keepdims=True)
            m_old = m_scr[g, rs, :]
            m_new = jnp.maximum(m_old, jnp.broadcast_to(blk_max, (rt, LANES)))
            corr = jnp.exp2(m_old - m_new)
            p_sum = jnp.zeros((rt, LANES), F32)
            for cs, lg_c in zip(cols, lg):
                p = jnp.exp2(lg_c - m_new)
                p_scr[g, rs, cs] = p.astype(BF16)
                p_sum = p_sum + p
            l_scr[g, rs, :] = corr * l_scr[g, rs, :] + p_sum
            corr_scr[g, rs, :] = corr
            m_scr[g, rs, :] = m_new

    def select_bias(j):
        for cs in cols:
            bias_scr[:, cs] = jnp.where(scr[j, :, cs] >= thr, 0.0, NEG)

    qk(0)
    qk(1)
    select_bias(0)
    softmax(0)
    pv(0)
    softmax(1)
    pv(1)
    for g in range(N_KV_HEADS):
        o = acc_scr[g] / jnp.sum(l_scr[g], axis=1, keepdims=True)
        for r in range(HEADS_PER_KV):
            hh = g * HEADS_PER_KV + r
            o_ref[0, :, hh * HEAD_DIM:(hh + 1) * HEAD_DIM] = o[r * tq:(r + 1) * tq].astype(o_ref.dtype)


def _dsa_sample(q, qi, kw, kic, ki4n, kc, kn, vc, vn):
    s_total, tq, qw = q.shape
    n_past = kic.shape[1]
    n_keys = n_past + tq
    kb = LANES * ((n_keys + LANES - 1) // LANES)
    topk = min(TOPK_MAX, n_keys // 4)
    kernel = functools.partial(_dsa_sample_kernel, tq=tq, kb=kb, topk=topk, n_keys=n_keys)
    bmap = lambda b: (b, 0, 0)
    rows_g = HEADS_PER_KV * tq
    return pl.pallas_call(
        kernel,
        grid=(s_total,),
        in_specs=[
            pl.BlockSpec((1, tq, qw), bmap),
            pl.BlockSpec((1, tq, IDX_COLS), bmap),
            pl.BlockSpec((1, tq, LANES), bmap),
            pl.BlockSpec((1, n_past, IDX_DIM), bmap),
            pl.BlockSpec((1, tq, 2 * LANES), bmap),
            pl.BlockSpec((1, N_KV_HEADS * n_past, HEAD_DIM), bmap),
            pl.BlockSpec((1, N_KV_HEADS * tq, HEAD_DIM), bmap),
            pl.BlockSpec((1, N_KV_HEADS * n_past, HEAD_DIM), bmap),
            pl.BlockSpec((1, N_KV_HEADS * tq, HEAD_DIM), bmap),
        ],
        out_specs=pl.BlockSpec((1, tq, qw), bmap),
        out_shape=jax.ShapeDtypeStruct((s_total, tq, qw), BF16),
        scratch_shapes=[
            pltpu.VMEM((1, tq, kb), F32),
            pltpu.VMEM((N_IDX_HEADS * tq, 2 * LANES), BF16),
            pltpu.VMEM((N_IDX_HEADS, tq, LANES), F32),
            pltpu.VMEM((N_KV_HEADS, rows_g, HEAD_DIM), BF16),
            pltpu.VMEM((N_KV_HEADS, rows_g, HEAD_DIM), F32),
            pltpu.VMEM((N_KV_HEADS, rows_g, LANES), F32),
            pltpu.VMEM((N_KV_HEADS, rows_g, LANES), F32),
            pltpu.VMEM((tq, LANES), F32),
            pltpu.VMEM((tq, LANES), F32),
            pltpu.VMEM((tq, LANES), F32),
            pltpu.VMEM((tq, LANES), F32),
            pltpu.VMEM((tq, kb), F32),
            pltpu.VMEM((N_KV_HEADS, rows_g, kb), F32),
            pltpu.VMEM((N_KV_HEADS, rows_g, kb), BF16),
            pltpu.VMEM((N_KV_HEADS, rows_g, LANES), F32),
            pltpu.VMEM((N_IDX_HEADS * tq, kb), F32),
            pltpu.VMEM((kb, 2 * LANES), BF16),
            pltpu.VMEM((N_KV_HEADS, kb, HEAD_DIM), BF16),
            pltpu.VMEM((N_KV_HEADS, kb, HEAD_DIM), BF16),
        ],
        compiler_params=_cparams(1),
        name="dsa_sample",
    )(q, qi, kw, kic, ki4n, kc, kn, vc, vn)


def _dsa_prompt_kernel(q_ref, qi_ref, kw_ref, ki4t_ref, kt_ref, v_ref, o_ref,
                       scr, qi4_scr, wb_scr, qa_scr, ka_scr, bias_scr, acc_scr, m_scr, l_scr,
                       lg_scr, p_scr, corr_scr, s_scr, thr_scr, *, kb, topk):
    tq = LANES
    qb = pl.program_id(1)
    lane_row = lax.broadcasted_iota(jnp.int32, (1, tq), 1)
    n_adm = ((qb * tq + lane_row) // CHUNK + 1) * CHUNK
    nkb = ((qb + 1) * tq + kb - 1) // kb
    lane128 = lax.broadcasted_iota(jnp.int32, (tq, LANES), 1)
    sub128 = lax.broadcasted_iota(jnp.int32, (LANES, tq), 0)
    nt = (((1,), (1,)), ((), ()))
    rows_g = HEADS_PER_KV * tq

    qi = qi_ref[0]
    for h in range(N_IDX_HEADS):
        tile = qi[:, (h // 2) * LANES:(h // 2 + 1) * LANES]
        own = (lane128 < IDX_DIM) if h % 2 == 0 else (lane128 >= IDX_DIM)
        both = jnp.where(own, tile, pltpu.roll(tile, IDX_DIM, 1))
        hi = both.astype(BF16).astype(F32)
        mix = jnp.where(lane128 < IDX_DIM, hi, both - hi).astype(BF16)
        qi4_scr[h * tq:(h + 1) * tq, 0:LANES] = mix
        qi4_scr[h * tq:(h + 1) * tq, LANES:2 * LANES] = mix
        wb_scr[h] = jnp.broadcast_to(kw_ref[0, :, WI_LANE + h:WI_LANE + h + 1], (tq, LANES))
    eye = jnp.where(sub128 == lane128, 1.0, 0.0).astype(BF16)
    for g in range(N_KV_HEADS):
        for r in range(HEADS_PER_KV):
            hh = g * HEADS_PER_KV + r
            qa_scr[g, r * tq:(r + 1) * tq, 0:HEAD_DIM] = (
                q_ref[0, :, hh * HEAD_DIM:(hh + 1) * HEAD_DIM])
            qa_scr[g, r * tq:(r + 1) * tq, HEAD_DIM:HEAD_DIM + tq] = eye

    n_blocks = ki4t_ref.shape[2] // kb

    def head_products(blk, slot):
        kblk = ki4t_ref[0, :, pl.ds(pl.multiple_of(blk * kb, kb), kb)]
        s_scr[slot] = jnp.dot(qi4_scr[...], kblk, preferred_element_type=F32)

    def reduce_heads(blk, slot, carry):
        rmin, rmax = carry
        for c in range(kb // LANES):
            cs = slice(c * LANES, (c + 1) * LANES)
            acc = jnp.zeros((tq, LANES), F32)
            for h in range(N_IDX_HEADS):
                acc = acc + wb_scr[h] * jnp.maximum(s_scr[slot, h * tq:(h + 1) * tq, cs], 0.0)
            acc_t = acc.T
            allowed = (blk * kb + c * LANES + sub128) < n_adm
            scr[blk, cs, :] = jnp.where(allowed, acc_t, NEG)
            rmin = jnp.minimum(rmin, jnp.where(allowed, acc_t, BIG))
            rmax = jnp.maximum(rmax, jnp.where(allowed, acc_t, -BIG))
        return rmin, rmax

    def score_step(i, carry, prefetch=True):
        second = jnp.minimum(2 * i + 1, n_blocks - 1)
        head_products(second, 1)
        carry = reduce_heads(2 * i, 0, carry)
        if prefetch:
            head_products(2 * i + 2, 0)
        return reduce_heads(second, 1, carry)

    trips = (nkb + 1) // 2
    head_products(0, 0)
    carry = lax.fori_loop(
        0, trips - 1, score_step,
        (jnp.full((LANES, tq), BIG, F32), jnp.full((LANES, tq), -BIG, F32)))
    rmin, rmax = score_step(trips - 1, carry, prefetch=False)
    rmin = jnp.min(rmin, axis=0, keepdims=True)
    rmax = jnp.max(rmax, axis=0, keepdims=True)

    def over_tiles(tile_fn, init, combine):
        def body(j, acc):
            for c in range(kb // LANES):
                kpos = (j * kb + c * LANES + sub128).astype(F32)
                acc = combine(acc, tile_fn(scr[j, c * LANES:(c + 1) * LANES, :], kpos))
            return acc

        return lax.fori_loop(0, nkb, body, jnp.full((LANES, tq), init, F32))

    def fold_rows(x, combine):
        n = x.shape[0]
        while n > 8:
            n //= 2
            x = combine(x[:n], x[n:])
        return x

    def count(pred_fn):
        part = fold_rows(over_tiles(pred_fn, 0.0, jnp.add), jnp.add)
        return jnp.sum(part, axis=0, keepdims=True)

    def count_ge(t):
        return count(lambda s, kpos: jnp.where(s >= t, 1.0, 0.0))

    def max_below(bound, strict):
        def tile(s, kpos):
            ok = (s < bound) if strict else (s <= bound)
            return jnp.where(ok, s, -BIG)

        part = fold_rows(over_tiles(tile, -BIG, jnp.maximum), jnp.maximum)
        return jnp.max(part, axis=0, keepdims=True)

    kf = np.float32(topk)
    few_f = np.float32(PEEL_POP)
    need = jnp.where(n_adm > topk, 1.0, 0.0)

    def bisect_pass(state):
        lo, hi, clo, chi, hit, thr = state
        mid = 0.5 * lo + 0.5 * hi
        cnt = count_ge(mid)
        ge = cnt >= kf
        exact = cnt == kf
        new = (jnp.where(ge, mid, lo), jnp.where(ge, hi, mid),
               jnp.where(ge, cnt, clo), jnp.where(ge, chi, cnt),
               jnp.where(exact, 1.0, hit),
               jnp.where(hit > 0.5, thr, jnp.where(exact, mid, thr)))
        moved = jnp.where(new[0] != lo, 1.0, jnp.where(new[1] != hi, 1.0, 0.0))
        few = (new[2] - new[3]) <= few_f
        pending = jnp.where(new[4] > 0.5, 0.0, jnp.where(few, 0.0, moved))
        return new, pending

    above_max = rmax + (jnp.abs(rmax) * 1e-6 + 1e-30)
    state = (rmin, above_max, n_adm.astype(F32), jnp.zeros((1, tq), F32), 1.0 - need, rmin)

    state, pending = lax.fori_loop(0, BISECT_PREFIX, lambda _, c: bisect_pass(c[0]),
                                   (state, jnp.ones((1, tq), F32)))

    def bis_cond(carry):
        return jnp.logical_and(carry[0] < BISECT_CAP, carry[1] > 0.0)

    def bis_body(carry):
        state, _ = bisect_pass(carry[2])
        state, pending = bisect_pass(state)
        return carry[0] + 2, jnp.max(pending), state

    _, _, state = lax.while_loop(bis_cond, bis_body,
                                 (jnp.int32(BISECT_PREFIX), jnp.max(pending), state))
    lo, hi, clo, chi, hit, thr = state

    open_f = jnp.where(hit > 0.5, 0.0, 1.0)
    few = jnp.where((clo - chi) <= few_f, open_f, 0.0) > 0.5
    rank = kf - chi
    t = hi
    for step in range(1, PEEL_POP):
        t = jnp.where(rank >= np.float32(step), max_below(t, True), t)
    peeled = jnp.where(few, jnp.where(count_ge(t) == kf, 1.0, 0.0), 0.0)
    thr = jnp.where(peeled > 0.5, t, thr)
    thr_scr[...] = thr
    tie_f = open_f * (1.0 - peeled)

    @pl.when(jnp.max(tie_f) > 0.0)
    def _ties():
        tie = tie_f > 0.5

        def short(c):
            return jnp.where(tie, jnp.where(c < kf, 1.0, 0.0), 0.0)

        def snap_cond(carry):
            return jnp.max(short(carry[1])) > 0.0

        def snap_body(carry):
            t_old, c_old = carry
            t_new = max_below(t_old, True)
            c_new = count_ge(t_new)
            upd = short(c_old) > 0.5
            return jnp.where(upd, t_new, t_old), jnp.where(upd, c_new, c_old)

        t0 = max_below(hi, False)
        t, _ = lax.while_loop(snap_cond, snap_body, (t0, count_ge(t0)))
        keep = kf - count(lambda s, kpos: jnp.where(s > t, 1.0, 0.0))

        def count_eq_before(cut):
            return count(lambda s, kpos: jnp.where(s == t, jnp.where(kpos < cut, 1.0, 0.0), 0.0))

        def cut_body(_, carry):
            clo, chi = carry
            cmid = jnp.floor(0.5 * (clo + chi))
            ok = count_eq_before(cmid) >= keep
            return jnp.where(ok, clo, cmid), jnp.where(ok, cmid, chi)

        n_total = np.float32(scr.shape[0] * kb)
        steps = int(np.ceil(np.log2(scr.shape[0] * kb))) + 1
        _, cut = lax.fori_loop(0, steps, cut_body, (jnp.zeros((1, tq), F32),
                                                    jnp.full((1, tq), n_total, F32)))
        cut = jnp.where(tie, cut, n_total)

        def drop(j, carry):
            for c in range(kb // LANES):
                rs = slice(c * LANES, (c + 1) * LANES)
                s = scr[j, rs, :]
                kpos = (j * kb + c * LANES + sub128).astype(F32)
                dropped = jnp.where(s == t, jnp.where(kpos >= cut, 1.0, 0.0), 0.0)
                scr[j, rs, :] = jnp.where(dropped > 0.5, NEG, s)
            return carry

        lax.fori_loop(0, nkb, drop, 0)
        thr_scr[...] = jnp.where(tie, t, thr)

    m_scr[...] = jnp.full(m_scr.shape, NEG, F32)
    l_scr[...] = jnp.zeros(l_scr.shape, F32)
    acc_scr[...] = jnp.zeros(acc_scr.shape, F32)
    thr = thr_scr[...]
    rt = ATT_ROW_TILE
    cols = [slice(c * LANES, (c + 1) * LANES) for c in range(kb // LANES)]

    def qk(j, g):
        koff = pl.ds(pl.multiple_of(j * kb, kb), kb)
        ka_scr[g, 0:HEAD_DIM, :] = kt_ref[0, g * HEAD_DIM:(g + 1) * HEAD_DIM, koff]
        if g == 0:
            for cs in cols:
                mask_t = jnp.where(scr[j, cs, :] >= thr, 0.0, NEG)
                mask = mask_t.T.astype(BF16)
                ka_scr[0, HEAD_DIM:HEAD_DIM + tq, cs] = mask
                bias_scr[:, cs] = mask
        else:
            ka_scr[g, HEAD_DIM:HEAD_DIM + tq, :] = bias_scr[...]
        lg_scr[g] = jnp.dot(qa_scr[g], ka_scr[g], preferred_element_type=F32)

    def pv(j, g):
        koff = pl.ds(pl.multiple_of(j * kb, kb), kb)
        vg = v_ref[0, koff, g * HEAD_DIM:(g + 1) * HEAD_DIM]
        acc_scr[g] = acc_scr[g] * corr_scr[g] + jnp.dot(p_scr[g], vg, preferred_element_type=F32)

    def softmax(g):
        for i in range(rows_g // rt):
            rs = slice(i * rt, (i + 1) * rt)
            lg = [lg_scr[g, rs, cs] for cs in cols]
            blk_max = jnp.max(functools.reduce(jnp.maximum, lg), axis=1, keepdims=True)
            m_old = m_scr[g, rs, :]
            m_new = jnp.maximum(m_old, jnp.broadcast_to(blk_max, (rt, LANES)))
            corr = jnp.exp2(m_old - m_new)
            p_sum = jnp.zeros((rt, LANES), F32)
            for cs, lg_c in zip(cols, lg):
                p = jnp.exp2(lg_c - m_new)
                p_scr[g, rs, cs] = p.astype(BF16)
                p_sum = p_sum + p
            l_scr[g, rs, :] = corr * l_scr[g, rs, :] + p_sum
            corr_scr[g, rs, :] = corr
            m_scr[g, rs, :] = m_new

    def attend(j, carry):
        pv(jnp.maximum(j - 1, 0), 1)
        qk(j, 1)
        softmax(0)
        pv(j, 0)
        qk(jnp.minimum(j + 1, nkb - 1), 0)
        softmax(1)
        return carry

    p_scr[1] = jnp.zeros(p_scr.shape[1:], BF16)
    corr_scr[1] = jnp.zeros(corr_scr.shape[1:], F32)
    qk(0, 0)
    lax.fori_loop(0, nkb, attend, 0)
    pv(nkb - 1, 1)
    for g in range(N_KV_HEADS):
        o = acc_scr[g] / jnp.sum(l_scr[g], axis=1, keepdims=True)
        for r in range(HEADS_PER_KV):
            hh = g * HEADS_PER_KV + r
            o_ref[0, :, hh * HEAD_DIM:(hh + 1) * HEAD_DIM] = o[r * tq:(r + 1) * tq].astype(o_ref.dtype)


def _dsa_prompt(q, qi, kw, ki4t, kt, v_bf, *, kb):
    s_total, t_q, qw = q.shape
    tq = LANES
    lp = ki4t.shape[2]
    assert lp == t_q and t_q % tq == 0 and lp % kb == 0 and kb % (2 * LANES) == 0
    topk = min(TOPK_MAX, lp // 4)
    kernel = functools.partial(_dsa_prompt_kernel, kb=kb, topk=topk)
    qmap = lambda b, j: (b, j, 0)
    kmap = lambda b, j: (b, 0, 0)
    rows_g = HEADS_PER_KV * tq
    stat = pltpu.VMEM((N_KV_HEADS, rows_g, LANES), F32)
    return pl.pallas_call(
        kernel,
        grid=(s_total, t_q // tq),
        in_specs=[
            pl.BlockSpec((1, tq, qw), qmap),
            pl.BlockSpec((1, tq, IDX_COLS), qmap),
            pl.BlockSpec((1, tq, LANES), qmap),
            pl.BlockSpec((1, 2 * LANES, lp), kmap, pipeline_mode=pl.Buffered(1)),
            pl.BlockSpec((1, kt.shape[1], lp), kmap, pipeline_mode=pl.Buffered(1)),
            pl.BlockSpec((1, lp, v_bf.shape[2]), kmap, pipeline_mode=pl.Buffered(1)),
        ],
        out_specs=pl.BlockSpec((1, tq, qw), qmap),
        out_shape=jax.ShapeDtypeStruct((s_total, t_q, qw), BF16),
        scratch_shapes=[
            pltpu.VMEM((lp // kb, kb, tq), F32),
            pltpu.VMEM((N_IDX_HEADS * tq, 2 * LANES), BF16),
            pltpu.VMEM((N_IDX_HEADS, tq, LANES), F32),
            pltpu.VMEM((N_KV_HEADS, rows_g, HEAD_DIM + tq), BF16),
            pltpu.VMEM((N_KV_HEADS, HEAD_DIM + tq, kb), BF16),
            pltpu.VMEM((tq, kb), BF16),
            pltpu.VMEM((N_KV_HEADS, rows_g, HEAD_DIM), F32),
            stat, stat,
            pltpu.VMEM((N_KV_HEADS, rows_g, kb), F32),
            pltpu.VMEM((N_KV_HEADS, rows_g, kb), BF16),
            stat,
            pltpu.VMEM((2, N_IDX_HEADS * tq, kb), F32),
            pltpu.VMEM((1, tq), F32),
        ],
        compiler_params=_cparams(2),
        name="dsa_prompt",
    )(q, qi, kw, ki4t, kt, v_bf)


def _mix_kernel(x_ref, gt_ref, gpost_ref, att_ref, u_ref, vn_ref, gates_ref, ws_ref, bs_ref,
                wpa_ref, wpb_ref, wout_ref, o_ref, sgu_scr):
    nb, t, d = x_ref.shape
    rows = nb * t
    gd = d // N_SGU_GROUPS
    dot = functools.partial(jnp.dot, preferred_element_type=F32)
    r = lax.broadcasted_iota(jnp.int32, (MLP_CHUNK, MLP_CHUNK), 0)
    c = lax.broadcasted_iota(jnp.int32, (MLP_CHUNK, MLP_CHUNK), 1)
    for g in range(N_SGU_GROUPS):
        wg = jnp.where(r >= c, ws_ref[g], 0.0).astype(BF16)
        gs = slice(g * gd, (g + 1) * gd)
        for ch in range(rows // MLP_CHUNK):
            rs = slice(ch * MLP_CHUNK, (ch + 1) * MLP_CHUNK)
            mixed = dot(wg, vn_ref[rs, gs].astype(BF16)) + bs_ref[:, gs]
            sgu_scr[rs, gs] = (u_ref[rs, gs].astype(F32) * mixed).astype(BF16)
    a = dot(att_ref[...], wpa_ref[...])
    b = dot(sgu_scr[...], wpb_ref[...])
    merged = gates_ref[:, 0:d].astype(F32) * a + gates_ref[:, d:2 * d].astype(F32) * b
    h = dot(merged.astype(BF16), wout_ref[...])
    hn = _rms(h, gpost_ref[...])
    o_ref[...] = x_ref[...] + gt_ref[...] * hn.reshape(nb, t, d)


def _row_blocking(s_total, t_full, nb):
    if nb == 1:
        t = min(t_full, 512)
        grid = (s_total, t_full // t)
        xmap = lambda b, j: (b, j, 0)
        mmap = lambda b, j: (b, 0, 0)
        rmap = lambda b, j: (b * (t_full // t) + j, 0)
    else:
        t = t_full
        grid = (s_total // nb,)
        xmap = lambda b: (b, 0, 0)
        mmap = lambda b: (b, 0, 0)
        rmap = lambda b: (b, 0)
    return t, grid, xmap, mmap, rmap


def _mix(x, gt, gpost, att, u, vn, gates, ws, bs, wpa, wpb, wout, nb):
    s_total, t_full, d = x.shape
    t, grid, xmap, mmap, rmap = _row_blocking(s_total, t_full, nb)
    rows = nb * t
    return pl.pallas_call(
        _mix_kernel,
        grid=grid,
        in_specs=[
            pl.BlockSpec((nb, t, d), xmap),
            pl.BlockSpec((nb, 1, d), mmap),
            _const_spec((1, 1, d)),
            pl.BlockSpec((rows, d), rmap),
            pl.BlockSpec((rows, d), rmap),
            pl.BlockSpec((rows, d), rmap),
            pl.BlockSpec((rows, 2 * d), rmap),
            _const_spec(ws.shape),
            _const_spec(bs.shape),
            _const_spec(wpa.shape),
            _const_spec(wpb.shape),
            _const_spec(wout.shape),
        ],
        out_specs=pl.BlockSpec((nb, t, d), xmap),
        out_shape=jax.ShapeDtypeStruct(x.shape, F32),
        scratch_shapes=[pltpu.VMEM((rows, d), BF16)],
        compiler_params=_cparams(len(grid)),
        name="mix",
    )(x, gt, gpost.reshape(1, 1, d), att, u, vn, gates, ws, bs, wpa, wpb, wout)


def _ffn_kernel(x_ref, sh_ref, sc_ref, gt_ref, gpre_ref, gpost_ref, win_ref, wout_ref, o_ref,
                *, n_split):
    nb, t, d = x_ref.shape
    rows = nb * t
    hidden = wout_ref.shape[0]
    hc = hidden // n_split
    dot = functools.partial(jnp.dot, preferred_element_type=F32)
    x = x_ref[...]
    xn = (_rms(x, gpre_ref[...]) * (1.0 + sc_ref[...]) + sh_ref[...]).reshape(rows, d).astype(BF16)
    acc = jnp.zeros((rows, d), F32)
    for i in range(n_split):
        gg = dot(xn, win_ref[:, i * hc:(i + 1) * hc])
        up = dot(xn, win_ref[:, hidden + i * hc:hidden + (i + 1) * hc])
        hmid = (gg * jax.nn.sigmoid(gg) * up).astype(BF16)
        acc = acc + dot(hmid, wout_ref[i * hc:(i + 1) * hc, :])
    hn = _rms(acc, gpost_ref[...])
    o_ref[...] = x + gt_ref[...] * hn.reshape(nb, t, d)


def _ffn(x, sh, sc, gt, gpre, gpost, win, wout, nb):
    s_total, t_full, d = x.shape
    t, grid, xmap, mmap, _ = _row_blocking(s_total, t_full, nb)
    hidden = wout.shape[0]
    n_split = 2
    assert hidden % (n_split * LANES) == 0
    return pl.pallas_call(
        functools.partial(_ffn_kernel, n_split=n_split),
        grid=grid,
        in_specs=[
            pl.BlockSpec((nb, t, d), xmap),
            pl.BlockSpec((nb, 1, d), mmap),
            pl.BlockSpec((nb, 1, d), mmap),
            pl.BlockSpec((nb, 1, d), mmap),
            _const_spec((1, 1, d)),
            _const_spec((1, 1, d)),
            _const_spec(win.shape),
            _const_spec(wout.shape),
        ],
        out_specs=pl.BlockSpec((nb, t, d), xmap),
        out_shape=jax.ShapeDtypeStruct(x.shape, F32),
        compiler_params=_cparams(len(grid)),
        name="ffn",
    )(x, sh, sc, gt, gpre.reshape(1, 1, d), gpost.reshape(1, 1, d), win, wout)


def _prep_weights(w_in, w_proj_a, w_proj_b, w_out, w_ffn_in, w_ffn_out, d):
    qw, kvw = N_HEADS * HEAD_DIM, N_KV_HEADS * HEAD_DIM
    o_q, o_k, o_v = 0, qw, qw + kvw
    o_qi = qw + 2 * kvw
    o_ki = o_qi + IDX_COLS
    o_wi = o_ki + IDX_DIM
    o_u = o_wi + N_IDX_HEADS
    wmain = jnp.concatenate([w_in[:, o_q:o_qi], w_in[:, o_u:]], axis=1).astype(BF16)
    pad = jnp.zeros((d, IDX_PAD - IDX_COLS - IDX_DIM - N_IDX_HEADS), F32)
    widx = jnp.concatenate([w_in[:, o_qi:o_u], pad], axis=1)
    wih, wil = _split_bf16(widx)
    return dict(
        wmain=wmain, wih=wih, wil=wil,
        wpa=w_proj_a.astype(BF16), wpb=w_proj_b.astype(BF16), wout=w_out.astype(BF16),
        wfi=w_ffn_in.astype(BF16), wfo=w_ffn_out.astype(BF16),
    )


def _spatial_tables(w_spatial, b_spatial, seq_len, d):
    cl = min(seq_len, MLP_CHUNK)
    reps = MLP_CHUNK // cl
    gd = d // N_SGU_GROUPS
    w = w_spatial[:, :cl, :cl]
    if reps > 1:
        eye = jnp.eye(reps, dtype=w.dtype)
        w = jnp.einsum("ab,gts->gatbs", eye, w).reshape(N_SGU_GROUPS, MLP_CHUNK, MLP_CHUNK)
    b = jnp.tile(b_spatial[:, :cl].T, (reps, 1))
    b = jnp.repeat(b, gd, axis=1)
    return w, b


def _stream(x, mod, pos, wts, g_pre_mix, g_post_mix, g_pre_ffn, g_post_ffn, b_gate, g_sgu,
            w_spatial, b_spatial, past, nb):
    s_total, t_full, d = x.shape
    sh1, sc1, gt1, sh2, sc2, gt2 = [m[:, None, :] for m in jnp.split(mod, 6, axis=-1)]

    idx_scale = np.float32((IDX_DIM * N_IDX_HEADS) ** -0.5)
    kw_scale = jnp.ones((LANES,), F32).at[WI_LANE:WI_LANE + N_IDX_HEADS].set(idx_scale)
    if nb == 1:
        tpos = pos
        tab_map = lambda j: (j, 0)
    else:
        tpos = jnp.tile(pos, nb)
        tab_map = lambda j: (0, 0)
    tabs = (_rope_tables(tpos, HEAD_DIM), _rope_tables(tpos, IDX_DIM), _kw_tables(tpos, kw_scale))

    vn_dtype = F32 if past is not None else BF16
    (q, k, v, k_bf, v_bf, qi, kw, ki4, u, vn, gates) = _proj(
        x, sh1, sc1, g_pre_mix, wts["wmain"], wts["wih"], wts["wil"], tabs, tab_map,
        b_gate, g_sgu, nb, vn_dtype)

    shp = lambda a: a.reshape(s_total, t_full, a.shape[-1])
    if past is None:
        n_keys = t_full
        key_major = lambda a: jnp.swapaxes(shp(a), 1, 2)
        att = _dsa_prompt(shp(q), shp(qi), shp(kw), key_major(ki4), key_major(k_bf), shp(v_bf),
                          kb=min(DSA_KEY_BLOCK, t_full))
    else:
        pk, pv, pki = past
        rows_view = lambda a: a.reshape(s_total, -1, HEAD_DIM)
        att = _dsa_sample(shp(q), shp(qi), shp(kw), pki, shp(ki4), rows_view(pk), rows_view(k),
                          rows_view(pv), rows_view(v))
    att = att.reshape(s_total * t_full, -1)

    ws, bs = _spatial_tables(w_spatial, b_spatial, t_full, d)
    x1 = _mix(x, gt1, g_post_mix, att, u, vn, gates, ws, bs,
              wts["wpa"], wts["wpb"], wts["wout"], nb)
    x2 = _ffn(x1, sh2, sc2, gt2, g_pre_ffn, g_post_ffn, wts["wfi"], wts["wfo"], nb)
    heads = lambda a: a.reshape(s_total, t_full, N_KV_HEADS, HEAD_DIM)
    return x2, heads(k), heads(v), shp(kw)[..., :IDX_DIM], shp(vn)


def _kw_tables(pos, scale_lanes):
    tab = _rope_tables(pos, IDX_DIM)
    lane = jnp.arange(LANES)
    c = jnp.where(lane[None, :] < IDX_DIM, tab[0], 1.0) * scale_lanes[None, :]
    s = jnp.where(lane[None, :] < IDX_DIM, tab[1], 0.0)
    return jnp.stack([c, s])


def kernel(x_prompt, x_sample, cache_k, cache_v, cache_idx_k, c_prompt, c_sample, w_ada, b_ada,
           g_pre_mix, g_post_mix, g_pre_ffn, g_post_ffn, w_in, b_gate, g_sgu, w_spatial,
           b_spatial, w_proj_a, w_proj_b, w_out, w_ffn_in, w_ffn_out):
    depth = w_ada.shape[0]
    bp, t_p, d = x_prompt.shape
    bs_, t_s, _ = x_sample.shape
    past_len = cache_k.shape[2]
    pos_p = jnp.arange(t_p, dtype=jnp.int32)
    pos_s = past_len + jnp.arange(t_s, dtype=jnp.int32)
    nb_s = max(1, min(bs_, 256 // t_s))

    xp, xs = x_prompt, x_sample
    outs = [[] for _ in range(7)]
    for l in range(depth):
        wts = _prep_weights(w_in[l], w_proj_a[l], w_proj_b[l], w_out[l], w_ffn_in[l],
                            w_ffn_out[l], d)
        c_all = jnp.concatenate([c_prompt, c_sample], axis=0)
        pad_rows = (-c_all.shape[0]) % 8
        c_all = jnp.pad(c_all, ((0, pad_rows), (0, 0)))
        mod = _ada(c_all, w_ada[l], b_ada[l])
        common = (wts, g_pre_mix[l], g_post_mix[l], g_pre_ffn[l], g_post_ffn[l], b_gate[l],
                  g_sgu[l], w_spatial[l], b_spatial[l])
        xp, kp, vp, kip, _ = _stream(xp, mod[:bp], pos_p, *common, None, 1)
        past = (cache_k[l], cache_v[l], cache_idx_k[l])
        xs, ks, vs, kis, vns = _stream(xs, mod[bp:bp + bs_], pos_s, *common, past, nb_s)
        for lst, val in zip(outs, (kp, vp, kip, ks, vs, kis, vns)):
            lst.append(val)
    return (xp, xs) + tuple(jnp.stack(o) for o in outs)
```
